```python
import jax, jax.numpy as jnp
from jax import lax
import numpy as np

D_MODEL = 1024
BATCH = 8
SEQ = 4096
DEPTH = 2
DEC_BATCH = 128
DEC_SEQ = 4
PAST_LEN = 16384
PAGE_SIZE = 128

CONV_DIM = 512
CONV_WIDTH = 31
POOL_DIM = 512
POOL_WINDOWS = (2, 4, 8, 16)
N_POOL_GROUPS = 4
POOL_GROUP_DIM = POOL_DIM // N_POOL_GROUPS
POOL_STATE = 15
N_HEADS = 8
N_KV_HEADS = 2
HEAD_DIM = 64
GROUP = N_HEADS // N_KV_HEADS
WINDOW = 128
BLOCK = 128
ATTN_DIM = N_HEADS * HEAD_DIM
KV_DIM = N_KV_HEADS * HEAD_DIM
N_BRANCHES = 3
IN_DIM = 2 * CONV_DIM + POOL_DIM + ATTN_DIM + 2 * KV_DIM + N_BRANCHES * D_MODEL
N_EXPERTS = 64
EXPERT_DIM = 256
TOP_K = 8
N_EXPERT_GROUPS = 8
TOPK_GROUPS = 4
ROUTED_SCALE = 2.5
SHARED_DIM = 256
LN_EPS = 1e-5
DEEPNORM_ALPHA = (2 * DEPTH) ** 0.25
DEEPNORM_BETA = (8 * DEPTH) ** -0.25

kernel_name = 'hybrid_conv_pool_swa_moe_deepnorm_step'


def layer_norm(x, g, b):
    xf = x.astype(jnp.float32)
    mu = xf.mean(-1, keepdims=True)
    var = jnp.square(xf - mu).mean(-1, keepdims=True)
    y = (xf - mu) * lax.rsqrt(var + LN_EPS) * g.astype(jnp.float32) + b.astype(jnp.float32)
    return y.astype(x.dtype)


def alibi_slopes():
    h = jnp.arange(1, N_HEADS + 1, dtype=jnp.float32)
    return (2.0 ** (-8.0 * h / N_HEADS)).reshape(N_KV_HEADS, GROUP, 1, 1)


def causal_dwconv(u_ext, w, b):
    out = lax.conv_general_dilated(
        u_ext, w[:, None, :].astype(u_ext.dtype), window_strides=(1,), padding='VALID',
        dimension_numbers=('NWC', 'WIO', 'NWC'), feature_group_count=u_ext.shape[-1])
    return out + b.astype(u_ext.dtype)


def multiscale_pool(u_ext, pos0, w_pool, scale):
    L = u_ext.shape[1] - POOL_STATE
    uf = u_ext.astype(jnp.float32)
    csum = jnp.concatenate([jnp.zeros_like(uf[:, :1]), lax.cumsum(uf, axis=1)], axis=1)
    end = csum[:, POOL_STATE + 1:]
    cur = uf[:, POOL_STATE:]
    t = pos0 + jnp.arange(L)
    outs = []
    for g, w in enumerate(POOL_WINDOWS):
        sl = slice(g * POOL_GROUP_DIM, (g + 1) * POOL_GROUP_DIM)
        start = csum[:, POOL_STATE + 1 - w:POOL_STATE + 1 - w + L, sl]
        cnt = jnp.minimum(w, t + 1).astype(jnp.float32)[None, :, None]
        outs.append((end[..., sl] - start) / cnt - cur[..., sl])
    pooled = jnp.stack(outs, axis=2).astype(u_ext.dtype)
    mixed = jnp.einsum('blgc,gcd->blgd', pooled, w_pool)
    return mixed.reshape(mixed.shape[0], L, POOL_DIM) * scale


def window_attention(q, k, v, dist, valid, sinks):
    s = jnp.einsum('...qhgd,...shd->...hgqs', q, k).astype(jnp.float32) * (HEAD_DIM ** -0.5)
    s = s - alibi_slopes() * dist[..., None, None, :, :]
    s = jnp.where(valid[..., None, None, :, :], s, -1e30)
    sink = jnp.broadcast_to(sinks.astype(jnp.float32).reshape(N_KV_HEADS, GROUP, 1, 1), s.shape[:-1] + (1,))
    p = jax.nn.softmax(jnp.concatenate([s, sink], axis=-1), axis=-1)[..., :-1]
    return jnp.einsum('...hgqs,...shd->...qhgd', p.astype(v.dtype), v)


def attn_prompt(q, k, v, sinks):
    B, L = q.shape[:2]
    nb = L // BLOCK
    qb = q.reshape(B, nb, BLOCK, N_KV_HEADS, GROUP, HEAD_DIM)

    def with_prev(xb):
        prev = jnp.concatenate([jnp.zeros_like(xb[:, :1]), xb[:, :-1]], axis=1)
        return jnp.concatenate([prev, xb], axis=2)

    kk = with_prev(k.reshape(B, nb, BLOCK, N_KV_HEADS, HEAD_DIM))
    vv = with_prev(v.reshape(B, nb, BLOCK, N_KV_HEADS, HEAD_DIM))
    i = jnp.arange(BLOCK)[:, None]
    j = jnp.arange(2 * BLOCK)[None, :]
    d = i + BLOCK - j
    key_pos = (jnp.arange(nb)[:, None, None] - 1) * BLOCK + j
    valid = (d >= 0) & (d <= WINDOW) & (key_pos >= 0)
    dist = jnp.broadcast_to(d, valid.shape).astype(jnp.float32)
    o = window_attention(qb, kk, vv, dist, valid, sinks)
    return o.reshape(B, L, ATTN_DIM)


def attn_sample(q, k, v, k_prev, v_prev, sinks):
    B, L = q.shape[:2]
    W_c = k_prev.shape[1]
    k_all = jnp.concatenate([k_prev, k], axis=1)
    v_all = jnp.concatenate([v_prev, v], axis=1)
    i = jnp.arange(L)[:, None]
    j = jnp.arange(W_c + L)[None, :]
    d = i + W_c - j
    valid = (d >= 0) & (d <= WINDOW)
    o = window_attention(q.reshape(B, L, N_KV_HEADS, GROUP, HEAD_DIM), k_all, v_all,
                         d.astype(jnp.float32), valid, sinks)
    return o.reshape(B, L, ATTN_DIM), k_all[:, -W_c:], v_all[:, -W_c:]


def moe_ffn(x, lp):
    logits = jnp.einsum('...d,de->...e', x, lp['w_router']).astype(jnp.float32)
    scores = jax.nn.sigmoid(logits)
    sel = scores + lp['router_bias'].astype(jnp.float32)
    per_group = N_EXPERTS // N_EXPERT_GROUPS
    grp = sel.reshape(sel.shape[:-1] + (N_EXPERT_GROUPS, per_group))
    grp_score = lax.top_k(grp, 2)[0].sum(-1)
    _, grp_idx = lax.top_k(grp_score, TOPK_GROUPS)
    grp_mask = jax.nn.one_hot(grp_idx, N_EXPERT_GROUPS, dtype=jnp.float32).sum(-2) > 0
    exp_mask = jnp.repeat(grp_mask, per_group, axis=-1)
    _, idx = lax.top_k(jnp.where(exp_mask, sel, -jnp.inf), TOP_K)
    wts = jnp.take_along_axis(scores, idx, axis=-1)
    wts = wts / wts.sum(-1, keepdims=True) * ROUTED_SCALE
    gate = (jax.nn.one_hot(idx, N_EXPERTS, dtype=jnp.float32) * wts[..., None]).sum(-2)
    hg = jnp.einsum('...d,edf->...ef', x, lp['w_e_gate'])
    hu = jnp.einsum('...d,edf->...ef', x, lp['w_e_up'])
    h = jax.nn.silu(hg) * hu * gate.astype(x.dtype)[..., None]
    routed = jnp.einsum('...ef,efd->...d', h, lp['w_e_down'])
    shared = (jax.nn.silu(x @ lp['w_s_gate']) * (x @ lp['w_s_up'])) @ lp['w_s_down']
    return routed + shared


def trunk_layer(x, conv_prev, pool_prev, k_prev, v_prev, pos0, lp):
    B, L, _ = x.shape
    h = jnp.einsum('bld,de->ble', x, lp['w_in'])
    o1 = 2 * CONV_DIM
    o2 = o1 + POOL_DIM
    o3 = o2 + ATTN_DIM
    o4 = o3 + KV_DIM
    o5 = o4 + KV_DIM
    u_conv = h[..., :CONV_DIM] * jax.nn.sigmoid(h[..., CONV_DIM:o1])
    conv_ext = jnp.concatenate([conv_prev, u_conv], axis=1)
    c = causal_dwconv(conv_ext, lp['conv_w'], lp['conv_b'])
    c = jax.nn.silu(layer_norm(c, lp['conv_ln_g'], lp['conv_ln_b']))
    br_a = c @ lp['w_a']
    new_conv = conv_ext[:, -(CONV_WIDTH - 1):]
    pool_ext = jnp.concatenate([pool_prev, h[..., o1:o2]], axis=1)
    br_b = multiscale_pool(pool_ext, pos0, lp['pool_w'], lp['pool_scale']) @ lp['w_b']
    new_pool = pool_ext[:, -POOL_STATE:]
    q = h[..., o2:o3]
    k = h[..., o3:o4].reshape(B, L, N_KV_HEADS, HEAD_DIM)
    v = h[..., o4:o5].reshape(B, L, N_KV_HEADS, HEAD_DIM)
    if k_prev is None:
        att = attn_prompt(q, k, v, lp['sinks'])
        new_k, new_v = k[:, -WINDOW:], v[:, -WINDOW:]
    else:
        att, new_k, new_v = attn_sample(q, k, v, k_prev, v_prev, lp['sinks'])
    br_c = att @ lp['w_c']
    g = jax.nn.sigmoid(h[..., o5:]).reshape(B, L, N_BRANCHES, D_MODEL)
    merged = g[..., 0, :] * br_a + g[..., 1, :] * br_b + g[..., 2, :] * br_c
    mix = merged @ lp['w_out']
    x = layer_norm(DEEPNORM_ALPHA * x + mix, lp['ln1_g'], lp['ln1_b'])
    if k_prev is None:
        f = lax.map(lambda row: moe_ffn(row, lp), x)
    else:
        f = moe_ffn(x, lp)
    x = layer_norm(DEEPNORM_ALPHA * x + f, lp['ln2_g'], lp['ln2_b'])
    return x, new_conv, new_pool, new_k, new_v


def setup_inputs(seed: int = 0) -> dict:
    key = jax.random.key(seed)
    ks = jax.random.split(key, 40)
    f32 = jnp.float32

    def nrm(k, shape, scale):
        return jax.random.normal(k, shape, f32) * scale

    win_cache = min(WINDOW, PAST_LEN)
    L_ = DEPTH
    return {
        'x_prompt': nrm(ks[0], (BATCH, SEQ, D_MODEL), 1.0),
        'x_sample': nrm(ks[1], (DEC_BATCH, DEC_SEQ, D_MODEL), 1.0),
        'cache_conv': nrm(ks[2], (L_, DEC_BATCH, CONV_WIDTH - 1, CONV_DIM), 0.5),
        'cache_pool': nrm(ks[3], (L_, DEC_BATCH, POOL_STATE, POOL_DIM), 1.0),
        'cache_k': nrm(ks[4], (L_, DEC_BATCH, win_cache, N_KV_HEADS, HEAD_DIM), 1.0),
        'cache_v': nrm(ks[5], (L_, DEC_BATCH, win_cache, N_KV_HEADS, HEAD_DIM), 1.0),
        'w_in': nrm(ks[6], (L_, D_MODEL, IN_DIM), D_MODEL ** -0.5),
        'conv_w': nrm(ks[7], (L_, CONV_WIDTH, CONV_DIM), CONV_WIDTH ** -0.5),
        'conv_b': nrm(ks[8], (L_, CONV_DIM), 0.02),
        'conv_ln_g': 1.0 + nrm(ks[9], (L_, CONV_DIM), 0.02),
        'conv_ln_b': nrm(ks[10], (L_, CONV_DIM), 0.02),
        'w_a': nrm(ks[11], (L_, CONV_DIM, D_MODEL), CONV_DIM ** -0.5),
        'pool_w': nrm(ks[12], (L_, N_POOL_GROUPS, POOL_GROUP_DIM, POOL_GROUP_DIM), POOL_GROUP_DIM ** -0.5),
        'pool_scale': 1.0 + nrm(ks[13], (L_, POOL_DIM), 0.1),
        'w_b': nrm(ks[14], (L_, POOL_DIM, D_MODEL), POOL_DIM ** -0.5),
        'attn_sinks': nrm(ks[15], (L_, N_HEADS), 1.0),
        'w_c': nrm(ks[16], (L_, ATTN_DIM, D_MODEL), ATTN_DIM ** -0.5),
        'w_out': nrm(ks[17], (L_, D_MODEL, D_MODEL), D_MODEL ** -0.5 * DEEPNORM_BETA),
        'ln1_g': 1.0 + nrm(ks[18], (L_, D_MODEL), 0.02),
        'ln1_b': nrm(ks[19], (L_, D_MODEL), 0.02),
        'w_router': nrm(ks[20], (L_, D_MODEL, N_EXPERTS), D_MODEL ** -0.5),
        'router_bias': nrm(ks[21], (L_, N_EXPERTS), 0.01),
        'w_e_gate': nrm(ks[22], (L_, N_EXPERTS, D_MODEL, EXPERT_DIM), D_MODEL ** -0.5),
        'w_e_up': nrm(ks[23], (L_, N_EXPERTS, D_MODEL, EXPERT_DIM), D_MODEL ** -0.5),
        'w_e_down': nrm(ks[24], (L_, N_EXPERTS, EXPERT_DIM, D_MODEL), EXPERT_DIM ** -0.5 * DEEPNORM_BETA),
        'w_s_gate': nrm(ks[25], (L_, D_MODEL, SHARED_DIM), D_MODEL ** -0.5),
        'w_s_up': nrm(ks[26], (L_, D_MODEL, SHARED_DIM), D_MODEL ** -0.5),
        'w_s_down': nrm(ks[27], (L_, SHARED_DIM, D_MODEL), SHARED_DIM ** -0.5 * DEEPNORM_BETA),
        'ln2_g': 1.0 + nrm(ks[28], (L_, D_MODEL), 0.02),
        'ln2_b': nrm(ks[29], (L_, D_MODEL), 0.02),
    }


def reference(x_prompt, x_sample, cache_conv, cache_pool, cache_k, cache_v,
              w_in, conv_w, conv_b, conv_ln_g, conv_ln_b, w_a, pool_w, pool_scale, w_b,
              attn_sinks, w_c, w_out, ln1_g, ln1_b, w_router, router_bias,
              w_e_gate, w_e_up, w_e_down, w_s_gate, w_s_up, w_s_down, ln2_g, ln2_b):
    yp, ys = x_prompt, x_sample
    Bp = x_prompt.shape[0]
    pc, pp, pk, pv, sc, sp, sk, sv = [], [], [], [], [], [], [], []
    for l in range(DEPTH):
        lp = {
            'w_in': w_in[l], 'conv_w': conv_w[l], 'conv_b': conv_b[l],
            'conv_ln_g': conv_ln_g[l], 'conv_ln_b': conv_ln_b[l], 'w_a': w_a[l],
            'pool_w': pool_w[l], 'pool_scale': pool_scale[l], 'w_b': w_b[l],
            'sinks': attn_sinks[l], 'w_c': w_c[l], 'w_out': w_out[l],
            'ln1_g': ln1_g[l], 'ln1_b': ln1_b[l], 'w_router': w_router[l], 'router_bias': router_bias[l],
            'w_e_gate': w_e_gate[l], 'w_e_up': w_e_up[l], 'w_e_down': w_e_down[l],
            'w_s_gate': w_s_gate[l], 'w_s_up': w_s_up[l], 'w_s_down': w_s_down[l],
            'ln2_g': ln2_g[l], 'ln2_b': ln2_b[l],
        }
        zc = jnp.zeros((Bp, CONV_WIDTH - 1, CONV_DIM), yp.dtype)
        zp = jnp.zeros((Bp, POOL_STATE, POOL_DIM), yp.dtype)
        yp, c1, p1, k1, v1 = trunk_layer(yp, zc, zp, None, None, 0, lp)
        ys, c2, p2, k2, v2 = trunk_layer(ys, cache_conv[l], cache_pool[l], cache_k[l], cache_v[l], PAST_LEN, lp)
        pc.append(c1); pp.append(p1); pk.append(k1); pv.append(v1)
        sc.append(c2); sp.append(p2); sk.append(k2); sv.append(v2)
    return (yp, ys, jnp.stack(pc), jnp.stack(pp), jnp.stack(pk), jnp.stack(pv),
            jnp.stack(sc), jnp.stack(sp), jnp.stack(sk), jnp.stack(sv))
```

```python
import functools
import math

import jax
import jax.numpy as jnp
from jax import lax
from jax.experimental import pallas as pl
from jax.experimental.pallas import tpu as pltpu

PAST_LEN = 16384
WINDOW = 128
BLOCK = 128
N_HEADS = 8
N_KV_HEADS = 2
GROUP = N_HEADS // N_KV_HEADS
HEAD_DIM = 64
POOL_WINDOWS = (2, 4, 8, 16)
N_EXPERT_GROUPS = 8
TOPK_GROUPS = 4
TOP_K = 8
ROUTED_SCALE = 2.5
LN_EPS = 1e-5
NEG_BIG = -1e30

LANES = 128
SUBLANES = 8
VMEM_LIMIT_BYTES = 56 * 1024 * 1024

BF16 = jnp.bfloat16
F32 = jnp.float32


def _dot(a, b):
    return jnp.dot(a, b, preferred_element_type=F32)


def _dot_nt(a, b):
    return lax.dot_general(a, b, (((1,), (1,)), ((), ())), preferred_element_type=F32)


def _layer_norm(x, g, b):
    mu = jnp.mean(x, axis=-1, keepdims=True)
    xc = x - mu
    var = jnp.mean(xc * xc, axis=-1, keepdims=True)
    return xc * lax.rsqrt(var + LN_EPS) * g + b


def _slope(head):
    return 2.0 ** (-8.0 * (head + 1) / N_HEADS)


def _dup_lanes(x, half):
    rolled = pltpu.roll(x, HEAD_DIM, axis=1)
    lane = lax.broadcasted_iota(jnp.int32, x.shape, 1)
    lo = lane < HEAD_DIM
    if half == 0:
        return jnp.where(lo, x, rolled)
    return jnp.where(lo, rolled, x)


def _attention_group(q_pairs, kk, vv, bias_fn, valid, sinks, heads):
    rows = q_pairs[0].shape[0]
    lane = lax.broadcasted_iota(jnp.int32, (rows, LANES), 1)
    lo = lane < HEAD_DIM
    stacked = []
    for qp in q_pairs:
        stacked.append(jnp.where(lo, qp, 0.0).astype(BF16))
        stacked.append(jnp.where(lo, 0.0, qp).astype(BF16))
    lhs = jnp.concatenate(stacked, axis=0)
    s_all = _dot_nt(lhs, kk)
    probs = []
    for a, head in enumerate(heads):
        s = s_all[a * rows:(a + 1) * rows] * (HEAD_DIM ** -0.5) - bias_fn(head)
        s = jnp.where(valid, s, NEG_BIG)
        sink = sinks[head]
        m = jnp.maximum(jnp.max(s, axis=-1, keepdims=True), sink)
        e = jnp.exp(s - m)
        denom = jnp.sum(e, axis=-1, keepdims=True) + jnp.exp(sink - m)
        probs.append((e / denom).astype(BF16))
    o_all = _dot(jnp.concatenate(probs, axis=0), vv)
    outs = []
    for pair in range(GROUP // 2):
        o_lo = o_all[(2 * pair) * rows:(2 * pair + 1) * rows]
        o_hi = o_all[(2 * pair + 1) * rows:(2 * pair + 2) * rows]
        outs.append(jnp.where(lo, o_lo, o_hi))
    return outs


def _pool_mix(pooled_groups, poolw_ref, pscale_ref):
    mixed = [_dot(p.astype(BF16), poolw_ref[g]) for g, p in enumerate(pooled_groups)]
    return jnp.concatenate(mixed, axis=-1) * pscale_ref[...]


def _merge_and_norm(x, xb, br_a, br_b, br_c, wg_ref, wout_ref, g_ref, b_ref, alpha):
    d_model = x.shape[-1]
    gates = jax.nn.sigmoid(_dot(xb, wg_ref[...]))
    merged = (gates[:, :d_model] * br_a + gates[:, d_model:2 * d_model] * br_b
              + gates[:, 2 * d_model:] * br_c)
    mix = _dot(merged.astype(BF16), wout_ref[...])
    return _layer_norm(alpha * x + mix, g_ref[...], b_ref[...])


def _prompt_mixer_kernel(sinks_ref, x_ref, wcv_ref, wpl_ref, wq_ref, wkv_ref, wg_ref,
                         convw_ref, convb_ref, clng_ref, clnb_ref, wa_ref,
                         poolw_ref, pscale_ref, wb_ref, wc_ref, wout_ref, ln1g_ref, ln1b_ref,
                         y_ref, oconv_ref, opool_ref, ok_ref, ov_ref,
                         ucat, pcat, kcat, vcat, *, tq, conv_hist, pool_hist, alpha):
    t = pl.program_id(1)
    conv_width = convw_ref.shape[0]
    conv_dim = convw_ref.shape[1]
    pool_dim = wpl_ref.shape[1]
    gdim = pool_dim // len(POOL_WINDOWS)
    conv_state = conv_width - 1
    pool_state = max(POOL_WINDOWS) - 1

    @pl.when(t == 0)
    def _():
        ucat[0:conv_hist, :] = jnp.zeros((conv_hist, conv_dim), F32)
        pcat[0:pool_hist, :] = jnp.zeros((pool_hist, pool_dim), F32)
        kcat[0:BLOCK, :] = jnp.zeros((BLOCK, LANES), F32)
        vcat[0:BLOCK, :] = jnp.zeros((BLOCK, LANES), F32)

    x = x_ref[0]
    xb = x.astype(BF16)

    hc = _dot(xb, wcv_ref[...])
    u = hc[:, :conv_dim] * jax.nn.sigmoid(hc[:, conv_dim:])
    ucat[conv_hist:conv_hist + tq, :] = u
    acc = jnp.zeros((tq, conv_dim), F32) + convb_ref[...]
    base = conv_hist - conv_state
    for j in range(conv_width):
        acc = acc + convw_ref[j:j + 1, :] * ucat[base + j:base + j + tq, :]
    c = _layer_norm(acc, clng_ref[...], clnb_ref[...])
    c = c * jax.nn.sigmoid(c)
    br_a = _dot(c.astype(BF16), wa_ref[...])
    oconv_ref[0] = ucat[conv_hist + tq - conv_state:conv_hist + tq, :]
    ucat[0:conv_hist, :] = ucat[tq:tq + conv_hist, :]

    hp = _dot(xb, wpl_ref[...])
    pcat[pool_hist:pool_hist + tq, :] = hp
    pos = t * tq + lax.broadcasted_iota(jnp.int32, (tq, gdim), 0)
    pooled = []
    for g, w in enumerate(POOL_WINDOWS):
        cols = slice(g * gdim, (g + 1) * gdim)
        s = pcat[pool_hist:pool_hist + tq, cols]
        cur = s
        for j in range(1, w):
            s = s + pcat[pool_hist - j:pool_hist - j + tq, cols]
        cnt = jnp.minimum(w, pos + 1).astype(F32)
        pooled.append(s / cnt - cur)
    br_b = _dot(_pool_mix(pooled, poolw_ref, pscale_ref).astype(BF16), wb_ref[...])
    opool_ref[0] = pcat[pool_hist + tq - pool_state:pool_hist + tq, :]
    pcat[0:pool_hist, :] = pcat[tq:tq + pool_hist, :]

    q = _dot(xb, wq_ref[...])
    kv = _dot(xb, wkv_ref[...])
    kcat[BLOCK:BLOCK + tq, :] = kv[:, :LANES]
    vcat[BLOCK:BLOCK + tq, :] = kv[:, LANES:]
    sinks = [sinks_ref[h] for h in range(N_HEADS)]
    qi = lax.broadcasted_iota(jnp.int32, (BLOCK, 2 * BLOCK), 0)
    kj = lax.broadcasted_iota(jnp.int32, (BLOCK, 2 * BLOCK), 1)
    dist_i = qi + BLOCK - kj
    dist = dist_i.astype(F32)
    in_window = (dist_i >= 0) & (dist_i <= WINDOW)
    first_key = jnp.where(t == 0, BLOCK, 0)
    att_blocks = []
    for b in range(tq // BLOCK):
        valid = in_window & (kj >= first_key) if b == 0 else in_window
        kblk = kcat[b * BLOCK:(b + 2) * BLOCK, :].astype(BF16)
        vblk = vcat[b * BLOCK:(b + 2) * BLOCK, :].astype(BF16)
        qblk = q[b * BLOCK:(b + 1) * BLOCK]
        cols = []
        for kvh in range(N_KV_HEADS):
            heads = [kvh * GROUP + a for a in range(GROUP)]
            q_pairs = [qblk[:, (kvh * GROUP + 2 * p) * HEAD_DIM:(kvh * GROUP + 2 * p + 2) * HEAD_DIM]
                       for p in range(GROUP // 2)]
            cols += _attention_group(q_pairs, _dup_lanes(kblk, kvh), _dup_lanes(vblk, kvh),
                                     lambda h: _slope(h) * dist, valid, sinks, heads)
        att_blocks.append(jnp.concatenate(cols, axis=-1))
    att = jnp.concatenate(att_blocks, axis=0)
    br_c = _dot(att.astype(BF16), wc_ref[...])
    ok_ref[0] = kcat[tq:tq + BLOCK, :]
    ov_ref[0] = vcat[tq:tq + BLOCK, :]
    kcat[0:BLOCK, :] = kcat[tq:tq + BLOCK, :]
    vcat[0:BLOCK, :] = vcat[tq:tq + BLOCK, :]

    y_ref[0] = _merge_and_norm(x, xb, br_a, br_b, br_c, wg_ref, wout_ref, ln1g_ref, ln1b_ref, alpha)


def _const_spec(arr):
    nd = arr.ndim
    return pl.BlockSpec(arr.shape, lambda *_: (0,) * nd)


def _prompt_mixer(x, sinks, wts, alpha, tq):
    batch, seq, d_model = x.shape
    conv_width, conv_dim = wts['conv_w'].shape
    pool_dim = wts['w_pl'].shape[1]
    conv_state = conv_width - 1
    pool_state = max(POOL_WINDOWS) - 1
    conv_hist = -(-conv_state // SUBLANES) * SUBLANES
    pool_hist = -(-pool_state // SUBLANES) * SUBLANES
    assert seq % tq == 0 and tq % BLOCK == 0 and tq >= max(conv_hist, pool_hist, BLOCK)
    names = ['w_cv', 'w_pl', 'w_q', 'w_kv', 'w_g', 'conv_w', 'conv_b', 'conv_ln_g', 'conv_ln_b', 'w_a',
             'pool_w', 'pool_scale', 'w_b', 'w_c', 'w_out', 'ln1_g', 'ln1_b']
    consts = [wts[n] for n in names]
    kern = functools.partial(_prompt_mixer_kernel, tq=tq, conv_hist=conv_hist, pool_hist=pool_hist,
                             alpha=alpha)
    out_shapes = (
        jax.ShapeDtypeStruct((batch, seq, d_model), F32),
        jax.ShapeDtypeStruct((batch, conv_state, conv_dim), F32),
        jax.ShapeDtypeStruct((batch, pool_state, pool_dim), F32),
        jax.ShapeDtypeStruct((batch, WINDOW, LANES), F32),
        jax.ShapeDtypeStruct((batch, WINDOW, LANES), F32),
    )
    return pl.pallas_call(
        kern,
        grid=(batch, seq // tq),
        in_specs=[pl.BlockSpec(memory_space=pltpu.SMEM),
                  pl.BlockSpec((1, tq, d_model), lambda b, t: (b, t, 0))]
                 + [_const_spec(a) for a in consts],
        out_specs=(
            pl.BlockSpec((1, tq, d_model), lambda b, t: (b, t, 0)),
            pl.BlockSpec((1, conv_state, conv_dim), lambda b, t: (b, 0, 0)),
            pl.BlockSpec((1, pool_state, pool_dim), lambda b, t: (b, 0, 0)),
            pl.BlockSpec((1, WINDOW, LANES), lambda b, t: (b, 0, 0)),
            pl.BlockSpec((1, WINDOW, LANES), lambda b, t: (b, 0, 0)),
        ),
        out_shape=out_shapes,
        scratch_shapes=[
            pltpu.VMEM((conv_hist + tq, conv_dim), F32),
            pltpu.VMEM((pool_hist + tq, pool_dim), F32),
            pltpu.VMEM((BLOCK + tq, LANES), F32),
            pltpu.VMEM((BLOCK + tq, LANES), F32),
        ],
        compiler_params=pltpu.CompilerParams(
            dimension_semantics=("arbitrary", "arbitrary"), vmem_limit_bytes=VMEM_LIMIT_BYTES),
        name="prompt_mixer",
    )(sinks, x, *consts)


def _sample_mixer_kernel(sinks_ref, x_ref, cconv_ref, cpool_ref, ck_ref, cv_ref,
                         wcv_ref, wpl_ref, wq_ref, wkv_ref, wg_ref,
                         convw_ref, convb_ref, clng_ref, clnb_ref, wa_ref,
                         poolw_ref, pscale_ref, wb_ref, wc_ref, wout_ref, ln1g_ref, ln1b_ref,
                         y_ref, ou_ref, op_ref, ok_ref, ov_ref, *, alpha):
    n_new, nb, d_model = x_ref.shape
    conv_width, conv_dim = convw_ref.shape
    conv_state = conv_width - 1
    pool_dim = wpl_ref.shape[1]
    gdim = pool_dim // len(POOL_WINDOWS)
    pool_state = max(POOL_WINDOWS) - 1
    win_cache = ck_ref.shape[1]
    rows = n_new * nb

    x = x_ref[...].reshape(rows, d_model)
    xb = x.astype(BF16)

    hc = _dot(xb, wcv_ref[...])
    u = hc[:, :conv_dim] * jax.nn.sigmoid(hc[:, conv_dim:])
    u_l = [u[l * nb:(l + 1) * nb] for l in range(n_new)]
    conv_rows = []
    for l in range(n_new):
        acc = jnp.zeros((nb, conv_dim), F32) + convb_ref[...]
        for j in range(conv_width):
            i = l + j
            src = cconv_ref[i] if i < conv_state else u_l[i - conv_state]
            acc = acc + convw_ref[j:j + 1, :] * src
        conv_rows.append(acc)
    c = _layer_norm(jnp.concatenate(conv_rows, axis=0), clng_ref[...], clnb_ref[...])
    c = c * jax.nn.sigmoid(c)
    br_a = _dot(c.astype(BF16), wa_ref[...])
    ou_ref[...] = u.reshape(n_new, nb, conv_dim)

    hp = _dot(xb, wpl_ref[...])
    hp_l = [hp[l * nb:(l + 1) * nb] for l in range(n_new)]
    pooled_rows = []
    for l in range(n_new):
        groups = []
        for g, w in enumerate(POOL_WINDOWS):
            cols = slice(g * gdim, (g + 1) * gdim)
            s = hp_l[l][:, cols]
            for j in range(1, w):
                i = l - j
                s = s + (hp_l[i][:, cols] if i >= 0 else cpool_ref[pool_state + i][:, cols])
            cnt = float(min(w, PAST_LEN + l + 1))
            groups.append(s / cnt - hp_l[l][:, cols])
        pooled_rows.append(groups)
    pooled = [jnp.concatenate([pooled_rows[l][g] for l in range(n_new)], axis=0)
              for g in range(len(POOL_WINDOWS))]
    br_b = _dot(_pool_mix(pooled, poolw_ref, pscale_ref).astype(BF16), wb_ref[...])
    op_ref[...] = hp.reshape(n_new, nb, pool_dim)

    q = _dot(xb, wq_ref[...])
    kv = _dot(xb, wkv_ref[...])
    k_new = kv[:, :LANES]
    v_new = kv[:, LANES:]
    ok_ref[...] = k_new.reshape(n_new, nb, LANES)
    ov_ref[...] = v_new.reshape(n_new, nb, LANES)
    n_cached = nb * win_cache
    pad = jnp.zeros((LANES - rows, LANES), BF16)
    k_all = jnp.concatenate([ck_ref[...].reshape(n_cached, LANES).astype(BF16), k_new.astype(BF16), pad], axis=0)
    v_all = jnp.concatenate([cv_ref[...].reshape(n_cached, LANES).astype(BF16), v_new.astype(BF16), pad], axis=0)
    n_keys = n_cached + LANES
    r = lax.broadcasted_iota(jnp.int32, (rows, n_keys), 0)
    col = lax.broadcasted_iota(jnp.int32, (rows, n_keys), 1)
    q_l = r // nb
    q_b = r % nb
    cached = col < n_cached
    cn = col - n_cached
    k_b = jnp.where(cached, col // win_cache, cn % nb)
    dist_i = jnp.where(cached, q_l + win_cache - col % win_cache, q_l - cn // nb)
    valid = (k_b == q_b) & (dist_i >= 0) & (dist_i <= WINDOW) & (cached | (cn < rows))
    dist = dist_i.astype(F32)
    sinks = [sinks_ref[h] for h in range(N_HEADS)]
    cols = []
    for kvh in range(N_KV_HEADS):
        heads = [kvh * GROUP + a for a in range(GROUP)]
        q_pairs = [q[:, (kvh * GROUP + 2 * p) * HEAD_DIM:(kvh * GROUP + 2 * p + 2) * HEAD_DIM]
                   for p in range(GROUP // 2)]
        cols += _attention_group(q_pairs, _dup_lanes(k_all, kvh), _dup_lanes(v_all, kvh),
                                 lambda h: _slope(h) * dist, valid, sinks, heads)
    att = jnp.concatenate(cols, axis=-1)
    br_c = _dot(att.astype(BF16), wc_ref[...])

    y = _merge_and_norm(x, xb, br_a, br_b, br_c, wg_ref, wout_ref, ln1g_ref, ln1b_ref, alpha)
    y_ref[...] = y.reshape(n_new, nb, d_model)


def _sample_mixer(x_lb, cconv_t, cpool_t, ck, cv, sinks, wts, alpha, nb):
    n_new, n_seq, d_model = x_lb.shape
    conv_state, _, conv_dim = cconv_t.shape
    pool_state, _, pool_dim = cpool_t.shape
    win_cache = ck.shape[1]
    assert n_seq % nb == 0 and nb % SUBLANES == 0 and n_new * nb <= LANES
    names = ['w_cv', 'w_pl', 'w_q', 'w_kv', 'w_g', 'conv_w', 'conv_b', 'conv_ln_g', 'conv_ln_b', 'w_a',
             'pool_w', 'pool_scale', 'w_b', 'w_c', 'w_out', 'ln1_g', 'ln1_b']
    consts = [wts[n] for n in names]

    def lb_spec(rows, width):
        return pl.BlockSpec((rows, nb, width), lambda c: (0, c, 0))

    out_shapes = (
        jax.ShapeDtypeStruct((n_new, n_seq, d_model), F32),
        jax.ShapeDtypeStruct((n_new, n_seq, conv_dim), F32),
        jax.ShapeDtypeStruct((n_new, n_seq, pool_dim), F32),
        jax.ShapeDtypeStruct((n_new, n_seq, LANES), F32),
        jax.ShapeDtypeStruct((n_new, n_seq, LANES), F32),
    )
    return pl.pallas_call(
        functools.partial(_sample_mixer_kernel, alpha=alpha),
        grid=(n_seq // nb,),
        in_specs=[pl.BlockSpec(memory_space=pltpu.SMEM),
                  lb_spec(n_new, d_model), lb_spec(conv_state, conv_dim), lb_spec(pool_state, pool_dim),
                  pl.BlockSpec((nb, win_cache, LANES), lambda c: (c, 0, 0)),
                  pl.BlockSpec((nb, win_cache, LANES), lambda c: (c, 0, 0))]
                 + [_const_spec(a) for a in consts],
        out_specs=(lb_spec(n_new, d_model), lb_spec(n_new, conv_dim), lb_spec(n_new, pool_dim),
                   lb_spec(n_new, LANES), lb_spec(n_new, LANES)),
        out_shape=out_shapes,
        compiler_params=pltpu.CompilerParams(
            dimension_semantics=("arbitrary",), vmem_limit_bytes=VMEM_LIMIT_BYTES),
        name="sample_mixer",
    )(sinks, x_lb, cconv_t, cpool_t, ck, cv, *consts)


def _route(x, wr_hi_ref, wr_lo_ref, rbias_ref):
    n_experts = wr_hi_ref.shape[0]
    per_group = n_experts // N_EXPERT_GROUPS
    tm = x.shape[0]
    x_hi = x.astype(BF16)
    x_lo = (x - x_hi.astype(F32)).astype(BF16)
    logits = (_dot_nt(wr_hi_ref[...], x_hi) + _dot_nt(wr_lo_ref[...], x_hi)
              + _dot_nt(wr_hi_ref[...], x_lo))
    scores = jax.nn.sigmoid(logits)
    sel = scores + rbias_ref[...]
    sub = lax.broadcasted_iota(jnp.int32, (per_group, tm), 0)
    grp_scores = []
    for g in range(N_EXPERT_GROUPS):
        v = sel[g * per_group:(g + 1) * per_group]
        m1 = jnp.max(v, axis=0, keepdims=True)
        first = jnp.min(jnp.where(v == m1, sub, per_group), axis=0, keepdims=True)
        m2 = jnp.max(jnp.where(sub == first, -jnp.inf, v), axis=0, keepdims=True)
        grp_scores.append(m1 + m2)
    keep = []
    for g in range(N_EXPERT_GROUPS):
        rank = jnp.zeros((1, tm), jnp.int32)
        for o in range(N_EXPERT_GROUPS):
            if o == g:
                continue
            ahead = (grp_scores[o] >= grp_scores[g]) if o < g else (grp_scores[o] > grp_scores[g])
            rank = rank + ahead.astype(jnp.int32)
        keep.append(jnp.broadcast_to(rank, (per_group, tm)))
    cand = jnp.where(jnp.concatenate(keep, axis=0) < TOPK_GROUPS, sel, -jnp.inf)
    eidx = lax.broadcasted_iota(jnp.int32, (n_experts, tm), 0)
    w = jnp.zeros((n_experts, tm), F32)
    for _ in range(TOP_K):
        m = jnp.max(cand, axis=0, keepdims=True)
        first = jnp.min(jnp.where(cand == m, eidx, n_experts), axis=0, keepdims=True)
        pick = eidx == first
        w = jnp.where(pick, scores, w)
        cand = jnp.where(pick, -jnp.inf, cand)
    gate_t = w / jnp.sum(w, axis=0, keepdims=True) * ROUTED_SCALE
    return gate_t.T


def _moe_kernel(x_ref, wrh_ref, wrl_ref, rbias_ref, weg_ref, weu_ref, wed_ref,
                wsg_ref, wsu_ref, wsd_ref, g_ref, b_ref, y_ref, xb_scr, gate_scr, acc_scr, *, alpha):
    e = pl.program_id(1)
    n_experts = pl.num_programs(1)

    @pl.when(e == 0)
    def _():
        x = x_ref[...]
        xb = x.astype(BF16)
        xb_scr[...] = xb
        gate_scr[...] = _route(x, wrh_ref, wrl_ref, rbias_ref)
        hs = jax.nn.silu(_dot(xb, wsg_ref[...])) * _dot(xb, wsu_ref[...])
        acc_scr[...] = _dot(hs.astype(BF16), wsd_ref[...])

    xb = xb_scr[...]
    hg = _dot(xb, weg_ref[0])
    hu = _dot(xb, weu_ref[0])
    gate = gate_scr[...]
    onehot = (lax.broadcasted_iota(jnp.int32, (gate.shape[1], LANES), 0) == e).astype(BF16)
    g_hi = gate.astype(BF16)
    r1 = gate - g_hi.astype(F32)
    g_mid = r1.astype(BF16)
    g_lo = (r1 - g_mid.astype(F32)).astype(BF16)
    gcol = _dot(g_hi, onehot) + _dot(g_mid, onehot) + _dot(g_lo, onehot)
    gcol = jnp.concatenate([gcol] * (hg.shape[1] // LANES), axis=-1)
    h = jax.nn.silu(hg) * hu * gcol
    acc_scr[...] += _dot(h.astype(BF16), wed_ref[0])

    @pl.when(e == n_experts - 1)
    def _():
        y_ref[...] = _layer_norm(alpha * x_ref[...] + acc_scr[...], g_ref[...], b_ref[...])


def _moe(x, wts, alpha, tm):
    n_tok, d_model = x.shape
    n_experts, _, expert_dim = wts['w_e_gate'].shape
    assert n_tok % tm == 0
    consts_a = [wts['wr_hi'], wts['wr_lo'], wts['router_bias']]
    consts_b = [wts['w_s_gate'], wts['w_s_up'], wts['w_s_down'], wts['ln2_g'], wts['ln2_b']]
    return pl.pallas_call(
        functools.partial(_moe_kernel, alpha=alpha),
        grid=(n_tok // tm, n_experts),
        in_specs=[pl.BlockSpec((tm, d_model), lambda i, e: (i, 0))]
                 + [_const_spec(a) for a in consts_a]
                 + [pl.BlockSpec((1, d_model, expert_dim), lambda i, e: (e, 0, 0)),
                    pl.BlockSpec((1, d_model, expert_dim), lambda i, e: (e, 0, 0)),
                    pl.BlockSpec((1, expert_dim, d_model), lambda i, e: (e, 0, 0))]
                 + [_const_spec(a) for a in consts_b],
        out_specs=pl.BlockSpec((tm, d_model), lambda i, e: (i, 0)),
        out_shape=jax.ShapeDtypeStruct((n_tok, d_model), F32),
        scratch_shapes=[pltpu.VMEM((tm, d_model), BF16),
                        pltpu.VMEM((tm, n_experts), F32),
                        pltpu.VMEM((tm, d_model), F32)],
        compiler_params=pltpu.CompilerParams(
            dimension_semantics=("arbitrary", "arbitrary"), vmem_limit_bytes=VMEM_LIMIT_BYTES),
        name="moe",
    )(x, *consts_a, wts['w_e_gate'], wts['w_e_up'], wts['w_e_down'], *consts_b)


def _layer_weights(l, w_in, conv_w, conv_b, conv_ln_g, conv_ln_b, w_a, pool_w, pool_scale, w_b, w_c, w_out,
                   ln1_g, ln1_b, w_router, router_bias, w_e_gate, w_e_up, w_e_down, w_s_gate, w_s_up,
                   w_s_down, ln2_g, ln2_b):
    conv_dim = conv_w.shape[-1]
    pool_dim = pool_scale.shape[-1]
    attn_dim = w_c.shape[1]
    kv_dim = N_KV_HEADS * HEAD_DIM
    o1 = 2 * conv_dim
    o2 = o1 + pool_dim
    o3 = o2 + attn_dim
    o5 = o3 + 2 * kv_dim
    wi = w_in[l]
    row = lambda v: v[l][None, :].astype(F32)
    wr_t = w_router[l].T
    wr_hi = wr_t.astype(BF16)
    return {
        'w_cv': wi[:, :o1].astype(BF16), 'w_pl': wi[:, o1:o2].astype(BF16), 'w_q': wi[:, o2:o3].astype(BF16),
        'w_kv': wi[:, o3:o5].astype(BF16), 'w_g': wi[:, o5:].astype(BF16),
        'conv_w': conv_w[l], 'conv_b': row(conv_b), 'conv_ln_g': row(conv_ln_g), 'conv_ln_b': row(conv_ln_b),
        'w_a': w_a[l].astype(BF16), 'pool_w': pool_w[l].astype(BF16), 'pool_scale': row(pool_scale),
        'w_b': w_b[l].astype(BF16), 'w_c': w_c[l].astype(BF16), 'w_out': w_out[l].astype(BF16),
        'ln1_g': row(ln1_g), 'ln1_b': row(ln1_b),
        'wr_hi': wr_hi, 'wr_lo': (wr_t - wr_hi.astype(F32)).astype(BF16),
        'router_bias': router_bias[l][:, None].astype(F32),
        'w_e_gate': w_e_gate[l].astype(BF16), 'w_e_up': w_e_up[l].astype(BF16),
        'w_e_down': w_e_down[l].astype(BF16),
        'w_s_gate': w_s_gate[l].astype(BF16), 'w_s_up': w_s_up[l].astype(BF16),
        'w_s_down': w_s_down[l].astype(BF16),
        'ln2_g': row(ln2_g), 'ln2_b': row(ln2_b),
    }


def _pick_tile(n, target):
    t = min(n, target)
    while n % t:
        t //= 2
    return t


def kernel(x_prompt, x_sample, cache_conv, cache_pool, cache_k, cache_v, w_in, conv_w, conv_b, conv_ln_g,
           conv_ln_b, w_a, pool_w, pool_scale, w_b, attn_sinks, w_c, w_out, ln1_g, ln1_b, w_router,
           router_bias, w_e_gate, w_e_up, w_e_down, w_s_gate, w_s_up, w_s_down, ln2_g, ln2_b):
    depth = w_in.shape[0]
    batch, seq, d_model = x_prompt.shape
    n_seq, n_new, _ = x_sample.shape
    win_cache = cache_k.shape[2]
    alpha = (2 * depth) ** 0.25
    tq = _pick_tile(seq, 512)
    nb = _pick_tile(n_seq, LANES // n_new // 2)
    tm_prompt = _pick_tile(batch * seq, 1024)
    tm_sample = _pick_tile(n_seq * n_new, 1024)

    yp = x_prompt
    ys = jnp.transpose(x_sample, (1, 0, 2))
    outs = [[] for _ in range(8)]
    for l in range(depth):
        wts = _layer_weights(l, w_in, conv_w, conv_b, conv_ln_g, conv_ln_b, w_a, pool_w, pool_scale, w_b,
                             w_c, w_out, ln1_g, ln1_b, w_router, router_bias, w_e_gate, w_e_up, w_e_down,
                             w_s_gate, w_s_up, w_s_down, ln2_g, ln2_b)
        sinks = attn_sinks[l].astype(F32)

        yp, c1, p1, k1, v1 = _prompt_mixer(yp, sinks, wts, alpha, tq)
        yp = _moe(yp.reshape(batch * seq, d_model), wts, alpha, tm_prompt).reshape(batch, seq, d_model)

        ck = cache_k[l].reshape(n_seq, win_cache, LANES)
        cv = cache_v[l].reshape(n_seq, win_cache, LANES)
        ys, u2, hp2, k2, v2 = _sample_mixer(
            ys, jnp.transpose(cache_conv[l], (1, 0, 2)), jnp.transpose(cache_pool[l], (1, 0, 2)),
            ck, cv, sinks, wts, alpha, nb)
        ys = _moe(ys.reshape(n_new * n_seq, d_model), wts, alpha, tm_sample).reshape(n_new, n_seq, d_model)

        to_bl = lambda a: jnp.transpose(a, (1, 0, 2))
        outs[0].append(c1)
        outs[1].append(p1)
        outs[2].append(k1.reshape(batch, WINDOW, N_KV_HEADS, HEAD_DIM))
        outs[3].append(v1.reshape(batch, WINDOW, N_KV_HEADS, HEAD_DIM))
        outs[4].append(jnp.concatenate([cache_conv[l], to_bl(u2)], axis=1)[:, n_new:])
        outs[5].append(jnp.concatenate([cache_pool[l], to_bl(hp2)], axis=1)[:, n_new:])
        outs[6].append(jnp.concatenate([ck, to_bl(k2)], axis=1)[:, n_new:]
                       .reshape(n_seq, win_cache, N_KV_HEADS, HEAD_DIM))
        outs[7].append(jnp.concatenate([cv, to_bl(v2)], axis=1)[:, n_new:]
                       .reshape(n_seq, win_cache, N_KV_HEADS, HEAD_DIM))
    return (yp, jnp.transpose(ys, (1, 0, 2))) + tuple(jnp.stack(o) for o in outs)
```

```python
import functools
import math

import jax
import jax.numpy as jnp
from jax import lax
from jax.experimental import pallas as pl
from jax.experimental.pallas import tpu as pltpu

PAST_LEN = 16384
WINDOW = 128
BLOCK = 128
N_HEADS = 8
N_KV_HEADS = 2
GROUP = N_HEADS // N_KV_HEADS
HEAD_DIM = 64
POOL_WINDOWS = (2, 4, 8, 16)
N_EXPERT_GROUPS = 8
TOPK_GROUPS = 4
TOP_K = 8
ROUTED_SCALE = 2.5
LN_EPS = 1e-5
NEG_BIG = -1e30

LANES = 128
SUBLANES = 8
VMEM_LIMIT_BYTES = 56 * 1024 * 1024
EXPERT_VMEM_LIMIT_BYTES = 60 * 1024 * 1024
EXPERT_ROW_TILE = 128

BF16 = jnp.bfloat16
F32 = jnp.float32


def _dot(a, b):
    return jnp.dot(a, b, preferred_element_type=F32)


def _dot_nt(a, b):
    return lax.dot_general(a, b, (((1,), (1,)), ((), ())), preferred_element_type=F32)


def _layer_norm(x, g, b):
    mu = jnp.mean(x, axis=-1, keepdims=True)
    xc = x - mu
    var = jnp.mean(xc * xc, axis=-1, keepdims=True)
    return xc * lax.rsqrt(var + LN_EPS) * g + b


def _slope(head):
    return 2.0 ** (-8.0 * (head + 1) / N_HEADS)


def _dup_lanes(x, half):
    rolled = pltpu.roll(x, HEAD_DIM, axis=1)
    lane = lax.broadcasted_iota(jnp.int32, x.shape, 1)
    lo = lane < HEAD_DIM
    if half == 0:
        return jnp.where(lo, x, rolled)
    return jnp.where(lo, rolled, x)


def _attention_group(q_pairs, kk, vv, bias_fn, valid, sinks, heads):
    rows = q_pairs[0].shape[0]
    lane = lax.broadcasted_iota(jnp.int32, (rows, LANES), 1)
    lo = lane < HEAD_DIM
    stacked = []
    for qp in q_pairs:
        stacked.append(jnp.where(lo, qp, 0.0).astype(BF16))
        stacked.append(jnp.where(lo, 0.0, qp).astype(BF16))
    lhs = jnp.concatenate(stacked, axis=0)
    s_all = _dot_nt(lhs, kk)
    probs = []
    for a, head in enumerate(heads):
        s = s_all[a * rows:(a + 1) * rows] * (HEAD_DIM ** -0.5) - bias_fn(head)
        s = jnp.where(valid, s, NEG_BIG)
        sink = sinks[head]
        m = jnp.maximum(jnp.max(s, axis=-1, keepdims=True), sink)
        e = jnp.exp(s - m)
        denom = jnp.sum(e, axis=-1, keepdims=True) + jnp.exp(sink - m)
        probs.append((e / denom).astype(BF16))
    o_all = _dot(jnp.concatenate(probs, axis=0), vv)
    outs = []
    for pair in range(GROUP // 2):
        o_lo = o_all[(2 * pair) * rows:(2 * pair + 1) * rows]
        o_hi = o_all[(2 * pair + 1) * rows:(2 * pair + 2) * rows]
        outs.append(jnp.where(lo, o_lo, o_hi))
    return outs


def _pool_mix(pooled_groups, poolw_ref, pscale_ref):
    mixed = [_dot(p.astype(BF16), poolw_ref[g]) for g, p in enumerate(pooled_groups)]
    return jnp.concatenate(mixed, axis=-1) * pscale_ref[...]


def _merge_and_norm(x, xb, br_a, br_b, br_c, wg_ref, wout_ref, g_ref, b_ref, alpha):
    d_model = x.shape[-1]
    gates = jax.nn.sigmoid(_dot(xb, wg_ref[...]))
    merged = (gates[:, :d_model] * br_a + gates[:, d_model:2 * d_model] * br_b
              + gates[:, 2 * d_model:] * br_c)
    mix = _dot(merged.astype(BF16), wout_ref[...])
    return _layer_norm(alpha * x + mix, g_ref[...], b_ref[...])


def _prompt_mixer_kernel(sinks_ref, x_ref, wcv_ref, wpl_ref, wq_ref, wkv_ref, wg_ref,
                         convw_ref, convb_ref, clng_ref, clnb_ref, wa_ref,
                         poolw_ref, pscale_ref, wb_ref, wc_ref, wout_ref, ln1g_ref, ln1b_ref,
                         y_ref, oconv_ref, opool_ref, ok_ref, ov_ref,
                         ucat, pcat, kcat, vcat, *, tq, conv_hist, pool_hist, alpha):
    t = pl.program_id(1)
    conv_width = convw_ref.shape[0]
    conv_dim = convw_ref.shape[1]
    pool_dim = wpl_ref.shape[1]
    gdim = pool_dim // len(POOL_WINDOWS)
    conv_state = conv_width - 1
    pool_state = max(POOL_WINDOWS) - 1

    @pl.when(t == 0)
    def _():
        ucat[0:conv_hist, :] = jnp.zeros((conv_hist, conv_dim), F32)
        pcat[0:pool_hist, :] = jnp.zeros((pool_hist, pool_dim), F32)
        kcat[0:BLOCK, :] = jnp.zeros((BLOCK, LANES), F32)
        vcat[0:BLOCK, :] = jnp.zeros((BLOCK, LANES), F32)

    x = x_ref[0]
    xb = x.astype(BF16)

    hc = _dot(xb, wcv_ref[...])
    u = hc[:, :conv_dim] * jax.nn.sigmoid(hc[:, conv_dim:])
    ucat[conv_hist:conv_hist + tq, :] = u
    acc = jnp.zeros((tq, conv_dim), F32) + convb_ref[...]
    base = conv_hist - conv_state
    for j in range(conv_width):
        acc = acc + convw_ref[j:j + 1, :] * ucat[base + j:base + j + tq, :]
    c = _layer_norm(acc, clng_ref[...], clnb_ref[...])
    c = c * jax.nn.sigmoid(c)
    br_a = _dot(c.astype(BF16), wa_ref[...])
    oconv_ref[0] = ucat[conv_hist + tq - conv_state:conv_hist + tq, :]
    ucat[0:conv_hist, :] = ucat[tq:tq + conv_hist, :]

    hp = _dot(xb, wpl_ref[...])
    pcat[pool_hist:pool_hist + tq, :] = hp
    pos = t * tq + lax.broadcasted_iota(jnp.int32, (tq, gdim), 0)
    pooled = []
    for g, w in enumerate(POOL_WINDOWS):
        cols = slice(g * gdim, (g + 1) * gdim)
        s = pcat[pool_hist:pool_hist + tq, cols]
        cur = s
        for j in range(1, w):
            s = s + pcat[pool_hist - j:pool_hist - j + tq, cols]
        cnt = jnp.minimum(w, pos + 1).astype(F32)
        pooled.append(s / cnt - cur)
    br_b = _dot(_pool_mix(pooled, poolw_ref, pscale_ref).astype(BF16), wb_ref[...])
    opool_ref[0] = pcat[pool_hist + tq - pool_state:pool_hist + tq, :]
    pcat[0:pool_hist, :] = pcat[tq:tq + pool_hist, :]

    q = _dot(xb, wq_ref[...])
    kv = _dot(xb, wkv_ref[...])
    kcat[BLOCK:BLOCK + tq, :] = kv[:, :LANES]
    vcat[BLOCK:BLOCK + tq, :] = kv[:, LANES:]
    sinks = [sinks_ref[h] for h in range(N_HEADS)]
    qi = lax.broadcasted_iota(jnp.int32, (BLOCK, 2 * BLOCK), 0)
    kj = lax.broadcasted_iota(jnp.int32, (BLOCK, 2 * BLOCK), 1)
    dist_i = qi + BLOCK - kj
    dist = dist_i.astype(F32)
    in_window = (dist_i >= 0) & (dist_i <= WINDOW)
    first_key = jnp.where(t == 0, BLOCK, 0)
    att_blocks = []
    for b in range(tq // BLOCK):
        valid = in_window & (kj >= first_key) if b == 0 else in_window
        kblk = kcat[b * BLOCK:(b + 2) * BLOCK, :].astype(BF16)
        vblk = vcat[b * BLOCK:(b + 2) * BLOCK, :].astype(BF16)
        qblk = q[b * BLOCK:(b + 1) * BLOCK]
        cols = []
        for kvh in range(N_KV_HEADS):
            heads = [kvh * GROUP + a for a in range(GROUP)]
            q_pairs = [qblk[:, (kvh * GROUP + 2 * p) * HEAD_DIM:(kvh * GROUP + 2 * p + 2) * HEAD_DIM]
                       for p in range(GROUP // 2)]
            cols += _attention_group(q_pairs, _dup_lanes(kblk, kvh), _dup_lanes(vblk, kvh),
                                     lambda h: _slope(h) * dist, valid, sinks, heads)
        att_blocks.append(jnp.concatenate(cols, axis=-1))
    att = jnp.concatenate(att_blocks, axis=0)
    br_c = _dot(att.astype(BF16), wc_ref[...])
    ok_ref[0] = kcat[tq:tq + BLOCK, :]
    ov_ref[0] = vcat[tq:tq + BLOCK, :]
    kcat[0:BLOCK, :] = kcat[tq:tq + BLOCK, :]
    vcat[0:BLOCK, :] = vcat[tq:tq + BLOCK, :]

    y_ref[0] = _merge_and_norm(x, xb, br_a, br_b, br_c, wg_ref, wout_ref, ln1g_ref, ln1b_ref, alpha)


def _const_spec(arr):
    nd = arr.ndim
    return pl.BlockSpec(arr.shape, lambda *_: (0,) * nd)


def _prompt_mixer(x, sinks, wts, alpha, tq):
    batch, seq, d_model = x.shape
    conv_width, conv_dim = wts['conv_w'].shape
    pool_dim = wts['w_pl'].shape[1]
    conv_state = conv_width - 1
    pool_state = max(POOL_WINDOWS) - 1
    conv_hist = -(-conv_state // SUBLANES) * SUBLANES
    pool_hist = -(-pool_state // SUBLANES) * SUBLANES
    assert seq % tq == 0 and tq % BLOCK == 0 and tq >= max(conv_hist, pool_hist, BLOCK)
    names = ['w_cv', 'w_pl', 'w_q', 'w_kv', 'w_g', 'conv_w', 'conv_b', 'conv_ln_g', 'conv_ln_b', 'w_a',
             'pool_w', 'pool_scale', 'w_b', 'w_c', 'w_out', 'ln1_g', 'ln1_b']
    consts = [wts[n] for n in names]
    kern = functools.partial(_prompt_mixer_kernel, tq=tq, conv_hist=conv_hist, pool_hist=pool_hist,
                             alpha=alpha)
    out_shapes = (
        jax.ShapeDtypeStruct((batch, seq, d_model), F32),
        jax.ShapeDtypeStruct((batch, conv_state, conv_dim), F32),
        jax.ShapeDtypeStruct((batch, pool_state, pool_dim), F32),
        jax.ShapeDtypeStruct((batch, WINDOW, LANES), F32),
        jax.ShapeDtypeStruct((batch, WINDOW, LANES), F32),
    )
    return pl.pallas_call(
        kern,
        grid=(batch, seq // tq),
        in_specs=[pl.BlockSpec(memory_space=pltpu.SMEM),
                  pl.BlockSpec((1, tq, d_model), lambda b, t: (b, t, 0))]
                 + [_const_spec(a) for a in consts],
        out_specs=(
            pl.BlockSpec((1, tq, d_model), lambda b, t: (b, t, 0)),
            pl.BlockSpec((1, conv_state, conv_dim), lambda b, t: (b, 0, 0)),
            pl.BlockSpec((1, pool_state, pool_dim), lambda b, t: (b, 0, 0)),
            pl.BlockSpec((1, WINDOW, LANES), lambda b, t: (b, 0, 0)),
            pl.BlockSpec((1, WINDOW, LANES), lambda b, t: (b, 0, 0)),
        ),
        out_shape=out_shapes,
        scratch_shapes=[
            pltpu.VMEM((conv_hist + tq, conv_dim), F32),
            pltpu.VMEM((pool_hist + tq, pool_dim), F32),
            pltpu.VMEM((BLOCK + tq, LANES), F32),
            pltpu.VMEM((BLOCK + tq, LANES), F32),
        ],
        compiler_params=pltpu.CompilerParams(
            dimension_semantics=("arbitrary", "arbitrary"), vmem_limit_bytes=VMEM_LIMIT_BYTES),
        name="prompt_mixer",
    )(sinks, x, *consts)


def _sample_mixer_kernel(sinks_ref, x_ref, cconv_ref, cpool_ref, ck_ref, cv_ref,
                         wcv_ref, wpl_ref, wq_ref, wkv_ref, wg_ref,
                         convw_ref, convb_ref, clng_ref, clnb_ref, wa_ref,
                         poolw_ref, pscale_ref, wb_ref, wc_ref, wout_ref, ln1g_ref, ln1b_ref,
                         y_ref, ou_ref, op_ref, ok_ref, ov_ref, *, alpha):
    n_new, nb, d_model = x_ref.shape
    conv_width, conv_dim = convw_ref.shape
    conv_state = conv_width - 1
    pool_dim = wpl_ref.shape[1]
    gdim = pool_dim // len(POOL_WINDOWS)
    pool_state = max(POOL_WINDOWS) - 1
    win_cache = ck_ref.shape[1]
    rows = n_new * nb

    x = x_ref[...].reshape(rows, d_model)
    xb = x.astype(BF16)

    hc = _dot(xb, wcv_ref[...])
    u = hc[:, :conv_dim] * jax.nn.sigmoid(hc[:, conv_dim:])
    u_l = [u[l * nb:(l + 1) * nb] for l in range(n_new)]
    conv_rows = []
    for l in range(n_new):
        acc = jnp.zeros((nb, conv_dim), F32) + convb_ref[...]
        for j in range(conv_width):
            i = l + j
            src = cconv_ref[i] if i < conv_state else u_l[i - conv_state]
            acc = acc + convw_ref[j:j + 1, :] * src
        conv_rows.append(acc)
    c = _layer_norm(jnp.concatenate(conv_rows, axis=0), clng_ref[...], clnb_ref[...])
    c = c * jax.nn.sigmoid(c)
    br_a = _dot(c.astype(BF16), wa_ref[...])
    ou_ref[...] = u.reshape(n_new, nb, conv_dim)

    hp = _dot(xb, wpl_ref[...])
    hp_l = [hp[l * nb:(l + 1) * nb] for l in range(n_new)]
    pooled_rows = []
    for l in range(n_new):
        groups = []
        for g, w in enumerate(POOL_WINDOWS):
            cols = slice(g * gdim, (g + 1) * gdim)
            s = hp_l[l][:, cols]
            for j in range(1, w):
                i = l - j
                s = s + (hp_l[i][:, cols] if i >= 0 else cpool_ref[pool_state + i][:, cols])
            cnt = float(min(w, PAST_LEN + l + 1))
            groups.append(s / cnt - hp_l[l][:, cols])
        pooled_rows.append(groups)
    pooled = [jnp.concatenate([pooled_rows[l][g] for l in range(n_new)], axis=0)
              for g in range(len(POOL_WINDOWS))]
    br_b = _dot(_pool_mix(pooled, poolw_ref, pscale_ref).astype(BF16), wb_ref[...])
    op_ref[...] = hp.reshape(n_new, nb, pool_dim)

    q = _dot(xb, wq_ref[...])
    kv = _dot(xb, wkv_ref[...])
    k_new = kv[:, :LANES]
    v_new = kv[:, LANES:]
    ok_ref[...] = k_new.reshape(n_new, nb, LANES)
    ov_ref[...] = v_new.reshape(n_new, nb, LANES)
    n_cached = nb * win_cache
    pad = jnp.zeros((LANES - rows, LANES), BF16)
    k_all = jnp.concatenate([ck_ref[...].reshape(n_cached, LANES).astype(BF16), k_new.astype(BF16), pad], axis=0)
    v_all = jnp.concatenate([cv_ref[...].reshape(n_cached, LANES).astype(BF16), v_new.astype(BF16), pad], axis=0)
    n_keys = n_cached + LANES
    r = lax.broadcasted_iota(jnp.int32, (rows, n_keys), 0)
    col = lax.broadcasted_iota(jnp.int32, (rows, n_keys), 1)
    q_l = r // nb
    q_b = r % nb
    cached = col < n_cached
    cn = col - n_cached
    k_b = jnp.where(cached, col // win_cache, cn % nb)
    dist_i = jnp.where(cached, q_l + win_cache - col % win_cache, q_l - cn // nb)
    valid = (k_b == q_b) & (dist_i >= 0) & (dist_i <= WINDOW) & (cached | (cn < rows))
    dist = dist_i.astype(F32)
    sinks = [sinks_ref[h] for h in range(N_HEADS)]
    cols = []
    for kvh in range(N_KV_HEADS):
        heads = [kvh * GROUP + a for a in range(GROUP)]
        q_pairs = [q[:, (kvh * GROUP + 2 * p) * HEAD_DIM:(kvh * GROUP + 2 * p + 2) * HEAD_DIM]
                   for p in range(GROUP // 2)]
        cols += _attention_group(q_pairs, _dup_lanes(k_all, kvh), _dup_lanes(v_all, kvh),
                                 lambda h: _slope(h) * dist, valid, sinks, heads)
    att = jnp.concatenate(cols, axis=-1)
    br_c = _dot(att.astype(BF16), wc_ref[...])

    y = _merge_and_norm(x, xb, br_a, br_b, br_c, wg_ref, wout_ref, ln1g_ref, ln1b_ref, alpha)
    y_ref[...] = y.reshape(n_new, nb, d_model)


def _sample_mixer(x_lb, cconv_t, cpool_t, ck, cv, sinks, wts, alpha, nb):
    n_new, n_seq, d_model = x_lb.shape
    conv_state, _, conv_dim = cconv_t.shape
    pool_state, _, pool_dim = cpool_t.shape
    win_cache = ck.shape[1]
    assert n_seq % nb == 0 and nb % SUBLANES == 0 and n_new * nb <= LANES
    names = ['w_cv', 'w_pl', 'w_q', 'w_kv', 'w_g', 'conv_w', 'conv_b', 'conv_ln_g', 'conv_ln_b', 'w_a',
             'pool_w', 'pool_scale', 'w_b', 'w_c', 'w_out', 'ln1_g', 'ln1_b']
    consts = [wts[n] for n in names]

    def lb_spec(rows, width):
        return pl.BlockSpec((rows, nb, width), lambda c: (0, c, 0))

    out_shapes = (
        jax.ShapeDtypeStruct((n_new, n_seq, d_model), F32),
        jax.ShapeDtypeStruct((n_new, n_seq, conv_dim), F32),
        jax.ShapeDtypeStruct((n_new, n_seq, pool_dim), F32),
        jax.ShapeDtypeStruct((n_new, n_seq, LANES), F32),
        jax.ShapeDtypeStruct((n_new, n_seq, LANES), F32),
    )
    return pl.pallas_call(
        functools.partial(_sample_mixer_kernel, alpha=alpha),
        grid=(n_seq // nb,),
        in_specs=[pl.BlockSpec(memory_space=pltpu.SMEM),
                  lb_spec(n_new, d_model), lb_spec(conv_state, conv_dim), lb_spec(pool_state, pool_dim),
                  pl.BlockSpec((nb, win_cache, LANES), lambda c: (c, 0, 0)),
                  pl.BlockSpec((nb, win_cache, LANES), lambda c: (c, 0, 0))]
                 + [_const_spec(a) for a in consts],
        out_specs=(lb_spec(n_new, d_model), lb_spec(n_new, conv_dim), lb_spec(n_new, pool_dim),
                   lb_spec(n_new, LANES), lb_spec(n_new, LANES)),
        out_shape=out_shapes,
        compiler_params=pltpu.CompilerParams(
            dimension_semantics=("arbitrary",), vmem_limit_bytes=VMEM_LIMIT_BYTES),
        name="sample_mixer",
    )(sinks, x_lb, cconv_t, cpool_t, ck, cv, *consts)


def _route(x, wr_hi_ref, wr_lo_ref, rbias_ref):
    n_experts = wr_hi_ref.shape[0]
    per_group = n_experts // N_EXPERT_GROUPS
    tm = x.shape[0]
    x_hi = x.astype(BF16)
    x_lo = (x - x_hi.astype(F32)).astype(BF16)
    logits = (_dot_nt(wr_hi_ref[...], x_hi) + _dot_nt(wr_lo_ref[...], x_hi)
              + _dot_nt(wr_hi_ref[...], x_lo))
    scores = jax.nn.sigmoid(logits)
    sel = scores + rbias_ref[...]
    sub = lax.broadcasted_iota(jnp.int32, (per_group, tm), 0)
    grp_scores = []
    for g in range(N_EXPERT_GROUPS):
        v = sel[g * per_group:(g + 1) * per_group]
        m1 = jnp.max(v, axis=0, keepdims=True)
        first = jnp.min(jnp.where(v == m1, sub, per_group), axis=0, keepdims=True)
        m2 = jnp.max(jnp.where(sub == first, -jnp.inf, v), axis=0, keepdims=True)
        grp_scores.append(m1 + m2)
    keep = []
    for g in range(N_EXPERT_GROUPS):
        rank = jnp.zeros((1, tm), jnp.int32)
        for o in range(N_EXPERT_GROUPS):
            if o == g:
                continue
            ahead = (grp_scores[o] >= grp_scores[g]) if o < g else (grp_scores[o] > grp_scores[g])
            rank = rank + ahead.astype(jnp.int32)
        keep.append(jnp.broadcast_to(rank, (per_group, tm)))
    cand = jnp.where(jnp.concatenate(keep, axis=0) < TOPK_GROUPS, sel, -jnp.inf)
    eidx = lax.broadcasted_iota(jnp.int32, (n_experts, tm), 0)
    chosen = jnp.zeros((n_experts, tm), F32)
    picks, weights = [], []
    for _ in range(TOP_K):
        m = jnp.max(cand, axis=0, keepdims=True)
        first = jnp.min(jnp.where(cand == m, eidx, n_experts), axis=0, keepdims=True)
        pick = eidx == first
        picks.append(first)
        weights.append(jnp.sum(jnp.where(pick, scores, 0.0), axis=0, keepdims=True))
        chosen = jnp.where(pick, 1.0, chosen)
        cand = jnp.where(pick, -jnp.inf, cand)
    total = weights[0]
    for w in weights[1:]:
        total = total + w
    gates = [w / total * ROUTED_SCALE for w in weights]
    return picks, gates, chosen


def _plan_kernel(x_ref, wrh_ref, wrl_ref, rbias_ref, gates_ref, pos_ref, cnt_ref, off_ref):
    n_experts = wrh_ref.shape[0]
    tm = x_ref.shape[0]
    picks, gates, chosen = _route(x_ref[...], wrh_ref, wrl_ref, rbias_ref)
    chosen_b = chosen.astype(BF16)
    before = (lax.broadcasted_iota(jnp.int32, (tm, tm), 0)
              < lax.broadcasted_iota(jnp.int32, (tm, tm), 1)).astype(BF16)
    rank_in_expert = _dot(chosen_b, before)
    lower = (lax.broadcasted_iota(jnp.int32, (n_experts, n_experts), 1)
             < lax.broadcasted_iota(jnp.int32, (n_experts, n_experts), 0)).astype(BF16)
    off = jnp.sum(_dot(lower, chosen_b), axis=1, keepdims=True)
    cnt = jnp.sum(chosen, axis=1, keepdims=True)
    rank = off + rank_in_expert
    eidx = lax.broadcasted_iota(jnp.int32, (n_experts, tm), 0)
    pos = [jnp.sum(jnp.where(eidx == p, rank, 0.0), axis=0, keepdims=True) for p in picks]
    gates_ref[0] = jnp.concatenate(gates, axis=0)
    pos_ref[0] = jnp.concatenate(pos, axis=0).astype(jnp.int32)
    cnt_ref[0] = jnp.broadcast_to(cnt, (n_experts, LANES)).astype(jnp.int32)
    off_ref[0] = jnp.broadcast_to(off, (n_experts, LANES)).astype(jnp.int32)


def _moe_plan(x, wts, chunk):
    n_tok, d_model = x.shape
    n_experts = wts['wr_hi'].shape[0]
    n_chunks = n_tok // chunk
    consts = [wts['wr_hi'], wts['wr_lo'], wts['router_bias']]
    gates, pos, cnt, off = pl.pallas_call(
        _plan_kernel,
        grid=(n_chunks,),
        in_specs=[pl.BlockSpec((chunk, d_model), lambda c: (c, 0))] + [_const_spec(a) for a in consts],
        out_specs=(pl.BlockSpec((1, TOP_K, chunk), lambda c: (c, 0, 0)),
                   pl.BlockSpec((1, TOP_K, chunk), lambda c: (c, 0, 0)),
                   pl.BlockSpec((1, n_experts, LANES), lambda c: (c, 0, 0)),
                   pl.BlockSpec((1, n_experts, LANES), lambda c: (c, 0, 0))),
        out_shape=(jax.ShapeDtypeStruct((n_chunks, TOP_K, chunk), F32),
                   jax.ShapeDtypeStruct((n_chunks, TOP_K, chunk), jnp.int32),
                   jax.ShapeDtypeStruct((n_chunks, n_experts, LANES), jnp.int32),
                   jax.ShapeDtypeStruct((n_chunks, n_experts, LANES), jnp.int32)),
        compiler_params=pltpu.CompilerParams(
            dimension_semantics=("arbitrary",), vmem_limit_bytes=VMEM_LIMIT_BYTES),
        name="moe_plan",
    )(x, *consts)
    return gates, pos, jnp.stack([cnt[:, :, 0], off[:, :, 0]], axis=1)


def _expert_kernel(pos_ref, gates_ref, seg_ref, x_ref, weg_ref, weu_ref, wed_ref, y_ref, sbuf, *, row_tile):
    e = pl.program_id(1)
    n_experts = pl.num_programs(1)
    tokens = x_ref.shape[0] // SUBLANES
    pairs = tokens * TOP_K

    def tile_at(ref, i):
        return ref.at[pl.ds(pl.multiple_of(i * SUBLANES, SUBLANES), SUBLANES), :]

    @pl.when(e == 0)
    def _():
        sbuf[pairs * SUBLANES:, :] = jnp.zeros((row_tile * SUBLANES, LANES), F32)

        def scatter(t, carry):
            row = tile_at(x_ref, t)[...]
            for k in range(TOP_K):
                tile_at(sbuf, pos_ref[0, k, t])[...] = row
            return carry
        lax.fori_loop(0, tokens, scatter, 0)

    n_rows = seg_ref[0, 0, e]
    first = seg_ref[0, 1, e]

    def run_tile(i, carry):
        base = pl.multiple_of((first + i * row_tile) * SUBLANES, SUBLANES)
        slabs = [sbuf[pl.ds(base + j, row_tile, stride=SUBLANES), :] for j in range(SUBLANES)]
        lhs = jnp.concatenate(slabs, axis=-1)
        lb = lhs.astype(BF16)
        h = jax.nn.silu(_dot(lb, weg_ref[0])) * _dot(lb, weu_ref[0])
        y = _dot(h.astype(BF16), wed_ref[0])
        live = lax.broadcasted_iota(jnp.int32, lhs.shape, 0) < n_rows - i * row_tile
        y = jnp.where(live, y, lhs)
        for j in range(SUBLANES):
            sbuf[pl.ds(base + j, row_tile, stride=SUBLANES), :] = y[:, j * LANES:(j + 1) * LANES]
        return carry
    lax.fori_loop(0, (n_rows + row_tile - 1) // row_tile, run_tile, 0)

    @pl.when(e == n_experts - 1)
    def _():
        def combine(t, carry):
            acc = jnp.zeros((SUBLANES, LANES), F32)
            for k in range(TOP_K):
                acc = acc + gates_ref[0, k, t] * tile_at(sbuf, pos_ref[0, k, t])[...]
            tile_at(y_ref, t)[...] = acc
            return carry
        lax.fori_loop(0, tokens, combine, 0)


def _moe_experts(x, gates, pos, seg, wts, chunk, row_tile):
    n_tok, d_model = x.shape
    n_experts, _, expert_dim = wts['w_e_gate'].shape
    assert d_model == SUBLANES * LANES and n_tok % chunk == 0
    rows = chunk * SUBLANES
    y = pl.pallas_call(
        functools.partial(_expert_kernel, row_tile=row_tile),
        grid=(n_tok // chunk, n_experts),
        in_specs=[pl.BlockSpec((1, TOP_K, chunk), lambda c, e: (c, 0, 0), memory_space=pltpu.SMEM),
                  pl.BlockSpec((1, TOP_K, chunk), lambda c, e: (c, 0, 0), memory_space=pltpu.SMEM),
                  pl.BlockSpec((1, 2, n_experts), lambda c, e: (c, 0, 0), memory_space=pltpu.SMEM),
                  pl.BlockSpec((rows, LANES), lambda c, e: (c, 0)),
                  pl.BlockSpec((1, d_model, expert_dim), lambda c, e: (e, 0, 0)),
                  pl.BlockSpec((1, d_model, expert_dim), lambda c, e: (e, 0, 0)),
                  pl.BlockSpec((1, expert_dim, d_model), lambda c, e: (e, 0, 0))],
        out_specs=pl.BlockSpec((rows, LANES), lambda c, e: (c, 0)),
        out_shape=jax.ShapeDtypeStruct((n_tok * SUBLANES, LANES), F32),
        scratch_shapes=[pltpu.VMEM(((chunk * TOP_K + row_tile) * SUBLANES, LANES), F32)],
        compiler_params=pltpu.CompilerParams(
            dimension_semantics=("arbitrary", "arbitrary"), vmem_limit_bytes=EXPERT_VMEM_LIMIT_BYTES),
        name="moe_experts",
    )(pos, gates, seg, x.reshape(n_tok * SUBLANES, LANES), wts['w_e_gate'], wts['w_e_up'], wts['w_e_down'])
    return y.reshape(n_tok, d_model)


def _finish_kernel(x_ref, routed_ref, wsg_ref, wsu_ref, wsd_ref, g_ref, b_ref, y_ref, *, alpha):
    x = x_ref[...]
    xb = x.astype(BF16)
    hs = jax.nn.silu(_dot(xb, wsg_ref[...])) * _dot(xb, wsu_ref[...])
    shared = _dot(hs.astype(BF16), wsd_ref[...])
    y_ref[...] = _layer_norm(alpha * x + (routed_ref[...] + shared), g_ref[...], b_ref[...])


def _moe_finish(x, routed, wts, alpha, tm):
    n_tok, d_model = x.shape
    consts = [wts['w_s_gate'], wts['w_s_up'], wts['w_s_down'], wts['ln2_g'], wts['ln2_b']]
    tok_spec = pl.BlockSpec((tm, d_model), lambda i: (i, 0))
    return pl.pallas_call(
        functools.partial(_finish_kernel, alpha=alpha),
        grid=(n_tok // tm,),
        in_specs=[tok_spec, tok_spec] + [_const_spec(a) for a in consts],
        out_specs=tok_spec,
        out_shape=jax.ShapeDtypeStruct((n_tok, d_model), F32),
        compiler_params=pltpu.CompilerParams(
            dimension_semantics=("arbitrary",), vmem_limit_bytes=VMEM_LIMIT_BYTES),
        name="moe_finish",
    )(x, routed, *consts)


def _moe(x, wts, alpha, chunk, row_tile):
    gates, pos, seg = _moe_plan(x, wts, chunk)
    routed = _moe_experts(x, gates, pos, seg, wts, chunk, row_tile)
    return _moe_finish(x, routed, wts, alpha, chunk)


def _layer_weights(l, w_in, conv_w, conv_b, conv_ln_g, conv_ln_b, w_a, pool_w, pool_scale, w_b, w_c, w_out,
                   ln1_g, ln1_b, w_router, router_bias, w_e_gate, w_e_up, w_e_down, w_s_gate, w_s_up,
                   w_s_down, ln2_g, ln2_b):
    conv_dim = conv_w.shape[-1]
    pool_dim = pool_scale.shape[-1]
    attn_dim = w_c.shape[1]
    kv_dim = N_KV_HEADS * HEAD_DIM
    o1 = 2 * conv_dim
    o2 = o1 + pool_dim
    o3 = o2 + attn_dim
    o5 = o3 + 2 * kv_dim
    wi = w_in[l]
    row = lambda v: v[l][None, :].astype(F32)
    wr_t = w_router[l].T
    wr_hi = wr_t.astype(BF16)
    return {
        'w_cv': wi[:, :o1].astype(BF16), 'w_pl': wi[:, o1:o2].astype(BF16), 'w_q': wi[:, o2:o3].astype(BF16),
        'w_kv': wi[:, o3:o5].astype(BF16), 'w_g': wi[:, o5:].astype(BF16),
        'conv_w': conv_w[l], 'conv_b': row(conv_b), 'conv_ln_g': row(conv_ln_g), 'conv_ln_b': row(conv_ln_b),
        'w_a': w_a[l].astype(BF16), 'pool_w': pool_w[l].astype(BF16), 'pool_scale': row(pool_scale),
        'w_b': w_b[l].astype(BF16), 'w_c': w_c[l].astype(BF16), 'w_out': w_out[l].astype(BF16),
        'ln1_g': row(ln1_g), 'ln1_b': row(ln1_b),
        'wr_hi': wr_hi, 'wr_lo': (wr_t - wr_hi.astype(F32)).astype(BF16),
        'router_bias': router_bias[l][:, None].astype(F32),
        'w_e_gate': w_e_gate[l].astype(BF16), 'w_e_up': w_e_up[l].astype(BF16),
        'w_e_down': w_e_down[l].astype(BF16),
        'w_s_gate': w_s_gate[l].astype(BF16), 'w_s_up': w_s_up[l].astype(BF16),
        'w_s_down': w_s_down[l].astype(BF16),
        'ln2_g': row(ln2_g), 'ln2_b': row(ln2_b),
    }


def _pick_tile(n, target):
    t = min(n, target)
    while n % t:
        t //= 2
    return t


def kernel(x_prompt, x_sample, cache_conv, cache_pool, cache_k, cache_v, w_in, conv_w, conv_b, conv_ln_g,
           conv_ln_b, w_a, pool_w, pool_scale, w_b, attn_sinks, w_c, w_out, ln1_g, ln1_b, w_router,
           router_bias, w_e_gate, w_e_up, w_e_down, w_s_gate, w_s_up, w_s_down, ln2_g, ln2_b):
    depth = w_in.shape[0]
    batch, seq, d_model = x_prompt.shape
    n_seq, n_new, _ = x_sample.shape
    win_cache = cache_k.shape[2]
    alpha = (2 * depth) ** 0.25
    tq = _pick_tile(seq, 512)
    nb = _pick_tile(n_seq, LANES // n_new // 2)
    tm_prompt = _pick_tile(batch * seq, 1024)
    tm_sample = _pick_tile(n_seq * n_new, 1024)

    yp = x_prompt
    ys = jnp.transpose(x_sample, (1, 0, 2))
    outs = [[] for _ in range(8)]
    for l in range(depth):
        wts = _layer_weights(l, w_in, conv_w, conv_b, conv_ln_g, conv_ln_b, w_a, pool_w, pool_scale, w_b,
                             w_c, w_out, ln1_g, ln1_b, w_router, router_bias, w_e_gate, w_e_up, w_e_down,
                             w_s_gate, w_s_up, w_s_down, ln2_g, ln2_b)
        sinks = attn_sinks[l].astype(F32)

        yp, c1, p1, k1, v1 = _prompt_mixer(yp, sinks, wts, alpha, tq)
        yp = _moe(yp.reshape(batch * seq, d_model), wts, alpha, tm_prompt, EXPERT_ROW_TILE).reshape(batch, seq, d_model)

        ck = cache_k[l].reshape(n_seq, win_cache, LANES)
        cv = cache_v[l].reshape(n_seq, win_cache, LANES)
        ys, u2, hp2, k2, v2 = _sample_mixer(
            ys, jnp.transpose(cache_conv[l], (1, 0, 2)), jnp.transpose(cache_pool[l], (1, 0, 2)),
            ck, cv, sinks, wts, alpha, nb)
        ys = _moe(ys.reshape(n_new * n_seq, d_model), wts, alpha, tm_sample, EXPERT_ROW_TILE).reshape(n_new, n_seq, d_model)

        to_bl = lambda a: jnp.transpose(a, (1, 0, 2))
        outs[0].append(c1)
        outs[1].append(p1)
        outs[2].append(k1.reshape(batch, WINDOW, N_KV_HEADS, HEAD_DIM))
        outs[3].append(v1.reshape(batch, WINDOW, N_KV_HEADS, HEAD_DIM))
        outs[4].append(jnp.concatenate([cache_conv[l], to_bl(u2)], axis=1)[:, n_new:])
        outs[5].append(jnp.concatenate([cache_pool[l], to_bl(hp2)], axis=1)[:, n_new:])
        outs[6].append(jnp.concatenate([ck, to_bl(k2)], axis=1)[:, n_new:]
                       .reshape(n_seq, win_cache, N_KV_HEADS, HEAD_DIM))
        outs[7].append(jnp.concatenate([cv, to_bl(v2)], axis=1)[:, n_new:]
                       .reshape(n_seq, win_cache, N_KV_HEADS, HEAD_DIM))
    return (yp, jnp.transpose(ys, (1, 0, 2))) + tuple(jnp.stack(o) for o in outs)
```

```python
import functools
import math

import jax
import jax.numpy as jnp
from jax import lax
from jax.experimental import pallas as pl
from jax.experimental.pallas import tpu as pltpu

PAST_LEN = 16384
WINDOW = 128
BLOCK = 128
N_HEADS = 8
N_KV_HEADS = 2
GROUP = N_HEADS // N_KV_HEADS
HEAD_DIM = 64
POOL_WINDOWS = (2, 4, 8, 16)
N_EXPERT_GROUPS = 8
TOPK_GROUPS = 4
TOP_K = 8
ROUTED_SCALE = 2.5
LN_EPS = 1e-5
NEG_BIG = -1e30

LANES = 128
SUBLANES = 8
VMEM_LIMIT_BYTES = 56 * 1024 * 1024
EXPERT_VMEM_LIMIT_BYTES = 60 * 1024 * 1024
BF16_ROWS = 16
EXPERTS_PER_STEP = 2
SCATTER_UNROLL = 8
COMBINE_UNROLL = 4

BF16 = jnp.bfloat16
F32 = jnp.float32


def _dot(a, b):
    return jnp.dot(a, b, preferred_element_type=F32)


def _dot_nt(a, b):
    return lax.dot_general(a, b, (((1,), (1,)), ((), ())), preferred_element_type=F32)


def _layer_norm(x, g, b):
    mu = jnp.mean(x, axis=-1, keepdims=True)
    xc = x - mu
    var = jnp.mean(xc * xc, axis=-1, keepdims=True)
    return xc * lax.rsqrt(var + LN_EPS) * g + b


def _slope(head):
    return 2.0 ** (-8.0 * (head + 1) / N_HEADS)


def _dup_lanes(x, half):
    rolled = pltpu.roll(x, HEAD_DIM, axis=1)
    lane = lax.broadcasted_iota(jnp.int32, x.shape, 1)
    lo = lane < HEAD_DIM
    if half == 0:
        return jnp.where(lo, x, rolled)
    return jnp.where(lo, rolled, x)


def _attention_group(q_pairs, kk, vv, bias_fn, valid, sinks, heads):
    rows = q_pairs[0].shape[0]
    lane = lax.broadcasted_iota(jnp.int32, (rows, LANES), 1)
    lo = lane < HEAD_DIM
    stacked = []
    for qp in q_pairs:
        stacked.append(jnp.where(lo, qp, 0.0).astype(BF16))
        stacked.append(jnp.where(lo, 0.0, qp).astype(BF16))
    lhs = jnp.concatenate(stacked, axis=0)
    s_all = _dot_nt(lhs, kk)
    probs = []
    for a, head in enumerate(heads):
        s = s_all[a * rows:(a + 1) * rows] * (HEAD_DIM ** -0.5) - bias_fn(head)
        s = jnp.where(valid, s, NEG_BIG)
        sink = sinks[head]
        m = jnp.maximum(jnp.max(s, axis=-1, keepdims=True), sink)
        e = jnp.exp(s - m)
        denom = jnp.sum(e, axis=-1, keepdims=True) + jnp.exp(sink - m)
        probs.append((e / denom).astype(BF16))
    o_all = _dot(jnp.concatenate(probs, axis=0), vv)
    outs = []
    for pair in range(GROUP // 2):
        o_lo = o_all[(2 * pair) * rows:(2 * pair + 1) * rows]
        o_hi = o_all[(2 * pair + 1) * rows:(2 * pair + 2) * rows]
        outs.append(jnp.where(lo, o_lo, o_hi))
    return outs


def _pool_mix(pooled_groups, poolw_ref, pscale_ref):
    mixed = [_dot(p.astype(BF16), poolw_ref[g]) for g, p in enumerate(pooled_groups)]
    return jnp.concatenate(mixed, axis=-1) * pscale_ref[...]


def _merge_and_norm(x, xb, br_a, br_b, br_c, wg_ref, wout_ref, g_ref, b_ref, alpha):
    d_model = x.shape[-1]
    gates = jax.nn.sigmoid(_dot(xb, wg_ref[...]))
    merged = (gates[:, :d_model] * br_a + gates[:, d_model:2 * d_model] * br_b
              + gates[:, 2 * d_model:] * br_c)
    mix = _dot(merged.astype(BF16), wout_ref[...])
    return _layer_norm(alpha * x + mix, g_ref[...], b_ref[...])


def _prompt_mixer_kernel(sinks_ref, x_ref, wcv_ref, wpl_ref, wq_ref, wkv_ref, wg_ref,
                         convw_ref, convb_ref, clng_ref, clnb_ref, wa_ref,
                         poolw_ref, pscale_ref, wb_ref, wc_ref, wout_ref, ln1g_ref, ln1b_ref,
                         y_ref, oconv_ref, opool_ref, ok_ref, ov_ref,
                         ucat, pcat, kcat, vcat, *, tq, conv_hist, pool_hist, alpha):
    t = pl.program_id(1)
    conv_width = convw_ref.shape[0]
    conv_dim = convw_ref.shape[1]
    pool_dim = wpl_ref.shape[1]
    gdim = pool_dim // len(POOL_WINDOWS)
    conv_state = conv_width - 1
    pool_state = max(POOL_WINDOWS) - 1

    @pl.when(t == 0)
    def _():
        ucat[0:conv_hist, :] = jnp.zeros((conv_hist, conv_dim), F32)
        pcat[0:pool_hist, :] = jnp.zeros((pool_hist, pool_dim), F32)
        kcat[0:BLOCK, :] = jnp.zeros((BLOCK, LANES), F32)
        vcat[0:BLOCK, :] = jnp.zeros((BLOCK, LANES), F32)

    x = x_ref[0]
    xb = x.astype(BF16)

    hc = _dot(xb, wcv_ref[...])
    u = hc[:, :conv_dim] * jax.nn.sigmoid(hc[:, conv_dim:])
    ucat[conv_hist:conv_hist + tq, :] = u
    acc = jnp.zeros((tq, conv_dim), F32) + convb_ref[...]
    base = conv_hist - conv_state
    for j in range(conv_width):
        acc = acc + convw_ref[j:j + 1, :] * ucat[base + j:base + j + tq, :]
    c = _layer_norm(acc, clng_ref[...], clnb_ref[...])
    c = c * jax.nn.sigmoid(c)
    br_a = _dot(c.astype(BF16), wa_ref[...])
    oconv_ref[0] = ucat[conv_hist + tq - conv_state:conv_hist + tq, :]
    ucat[0:conv_hist, :] = ucat[tq:tq + conv_hist, :]

    hp = _dot(xb, wpl_ref[...])
    pcat[pool_hist:pool_hist + tq, :] = hp
    pos = t * tq + lax.broadcasted_iota(jnp.int32, (tq, gdim), 0)
    pooled = []
    for g, w in enumerate(POOL_WINDOWS):
        cols = slice(g * gdim, (g + 1) * gdim)
        s = pcat[pool_hist:pool_hist + tq, cols]
        cur = s
        for j in range(1, w):
            s = s + pcat[pool_hist - j:pool_hist - j + tq, cols]
        cnt = jnp.minimum(w, pos + 1).astype(F32)
        pooled.append(s / cnt - cur)
    br_b = _dot(_pool_mix(pooled, poolw_ref, pscale_ref).astype(BF16), wb_ref[...])
    opool_ref[0] = pcat[pool_hist + tq - pool_state:pool_hist + tq, :]
    pcat[0:pool_hist, :] = pcat[tq:tq + pool_hist, :]

    q = _dot(xb, wq_ref[...])
    kv = _dot(xb, wkv_ref[...])
    kcat[BLOCK:BLOCK + tq, :] = kv[:, :LANES]
    vcat[BLOCK:BLOCK + tq, :] = kv[:, LANES:]
    sinks = [sinks_ref[h] for h in range(N_HEADS)]
    qi = lax.broadcasted_iota(jnp.int32, (BLOCK, 2 * BLOCK), 0)
    kj = lax.broadcasted_iota(jnp.int32, (BLOCK, 2 * BLOCK), 1)
    dist_i = qi + BLOCK - kj
    dist = dist_i.astype(F32)
    in_window = (dist_i >= 0) & (dist_i <= WINDOW)
    first_key = jnp.where(t == 0, BLOCK, 0)
    att_blocks = []
    for b in range(tq // BLOCK):
        valid = in_window & (kj >= first_key) if b == 0 else in_window
        kblk = kcat[b * BLOCK:(b + 2) * BLOCK, :].astype(BF16)
        vblk = vcat[b * BLOCK:(b + 2) * BLOCK, :].astype(BF16)
        qblk = q[b * BLOCK:(b + 1) * BLOCK]
        cols = []
        for kvh in range(N_KV_HEADS):
            heads = [kvh * GROUP + a for a in range(GROUP)]
            q_pairs = [qblk[:, (kvh * GROUP + 2 * p) * HEAD_DIM:(kvh * GROUP + 2 * p + 2) * HEAD_DIM]
                       for p in range(GROUP // 2)]
            cols += _attention_group(q_pairs, _dup_lanes(kblk, kvh), _dup_lanes(vblk, kvh),
                                     lambda h: _slope(h) * dist, valid, sinks, heads)
        att_blocks.append(jnp.concatenate(cols, axis=-1))
    att = jnp.concatenate(att_blocks, axis=0)
    br_c = _dot(att.astype(BF16), wc_ref[...])
    ok_ref[0] = kcat[tq:tq + BLOCK, :]
    ov_ref[0] = vcat[tq:tq + BLOCK, :]
    kcat[0:BLOCK, :] = kcat[tq:tq + BLOCK, :]
    vcat[0:BLOCK, :] = vcat[tq:tq + BLOCK, :]

    y_ref[0] = _merge_and_norm(x, xb, br_a, br_b, br_c, wg_ref, wout_ref, ln1g_ref, ln1b_ref, alpha)


def _const_spec(arr):
    nd = arr.ndim
    return pl.BlockSpec(arr.shape, lambda *_: (0,) * nd)


def _prompt_mixer(x, sinks, wts, alpha, tq):
    batch, seq, d_model = x.shape
    conv_width, conv_dim = wts['conv_w'].shape
    pool_dim = wts['w_pl'].shape[1]
    conv_state = conv_width - 1
    pool_state = max(POOL_WINDOWS) - 1
    conv_hist = -(-conv_state // SUBLANES) * SUBLANES
    pool_hist = -(-pool_state // SUBLANES) * SUBLANES
    assert seq % tq == 0 and tq % BLOCK == 0 and tq >= max(conv_hist, pool_hist, BLOCK)
    names = ['w_cv', 'w_pl', 'w_q', 'w_kv', 'w_g', 'conv_w', 'conv_b', 'conv_ln_g', 'conv_ln_b', 'w_a',
             'pool_w', 'pool_scale', 'w_b', 'w_c', 'w_out', 'ln1_g', 'ln1_b']
    consts = [wts[n] for n in names]
    kern = functools.partial(_prompt_mixer_kernel, tq=tq, conv_hist=conv_hist, pool_hist=pool_hist,
                             alpha=alpha)
    out_shapes = (
        jax.ShapeDtypeStruct((batch, seq, d_model), F32),
        jax.ShapeDtypeStruct((batch, conv_state, conv_dim), F32),
        jax.ShapeDtypeStruct((batch, pool_state, pool_dim), F32),
        jax.ShapeDtypeStruct((batch, WINDOW, LANES), F32),
        jax.ShapeDtypeStruct((batch, WINDOW, LANES), F32),
    )
    return pl.pallas_call(
        kern,
        grid=(batch, seq // tq),
        in_specs=[pl.BlockSpec(memory_space=pltpu.SMEM),
                  pl.BlockSpec((1, tq, d_model), lambda b, t: (b, t, 0))]
                 + [_const_spec(a) for a in consts],
        out_specs=(
            pl.BlockSpec((1, tq, d_model), lambda b, t: (b, t, 0)),
            pl.BlockSpec((1, conv_state, conv_dim), lambda b, t: (b, 0, 0)),
            pl.BlockSpec((1, pool_state, pool_dim), lambda b, t: (b, 0, 0)),
            pl.BlockSpec((1, WINDOW, LANES), lambda b, t: (b, 0, 0)),
            pl.BlockSpec((1, WINDOW, LANES), lambda b, t: (b, 0, 0)),
        ),
        out_shape=out_shapes,
        scratch_shapes=[
            pltpu.VMEM((conv_hist + tq, conv_dim), F32),
            pltpu.VMEM((pool_hist + tq, pool_dim), F32),
            pltpu.VMEM((BLOCK + tq, LANES), F32),
            pltpu.VMEM((BLOCK + tq, LANES), F32),
        ],
        compiler_params=pltpu.CompilerParams(
            dimension_semantics=("arbitrary", "arbitrary"), vmem_limit_bytes=VMEM_LIMIT_BYTES),
        name="prompt_mixer",
    )(sinks, x, *consts)


def _sample_mixer_kernel(sinks_ref, x_ref, cconv_ref, cpool_ref, ck_ref, cv_ref,
                         wcv_ref, wpl_ref, wq_ref, wkv_ref, wg_ref,
                         convw_ref, convb_ref, clng_ref, clnb_ref, wa_ref,
                         poolw_ref, pscale_ref, wb_ref, wc_ref, wout_ref, ln1g_ref, ln1b_ref,
                         y_ref, ou_ref, op_ref, ok_ref, ov_ref, *, alpha):
    n_new, nb, d_model = x_ref.shape
    conv_width, conv_dim = convw_ref.shape
    conv_state = conv_width - 1
    pool_dim = wpl_ref.shape[1]
    gdim = pool_dim // len(POOL_WINDOWS)
    pool_state = max(POOL_WINDOWS) - 1
    win_cache = ck_ref.shape[1]
    rows = n_new * nb

    x = x_ref[...].reshape(rows, d_model)
    xb = x.astype(BF16)

    hc = _dot(xb, wcv_ref[...])
    u = hc[:, :conv_dim] * jax.nn.sigmoid(hc[:, conv_dim:])
    u_l = [u[l * nb:(l + 1) * nb] for l in range(n_new)]
    conv_rows = []
    for l in range(n_new):
        acc = jnp.zeros((nb, conv_dim), F32) + convb_ref[...]
        for j in range(conv_width):
            i = l + j
            src = cconv_ref[i] if i < conv_state else u_l[i - conv_state]
            acc = acc + convw_ref[j:j + 1, :] * src
        conv_rows.append(acc)
    c = _layer_norm(jnp.concatenate(conv_rows, axis=0), clng_ref[...], clnb_ref[...])
    c = c * jax.nn.sigmoid(c)
    br_a = _dot(c.astype(BF16), wa_ref[...])
    ou_ref[...] = u.reshape(n_new, nb, conv_dim)

    hp = _dot(xb, wpl_ref[...])
    hp_l = [hp[l * nb:(l + 1) * nb] for l in range(n_new)]
    pooled_rows = []
    for l in range(n_new):
        groups = []
        for g, w in enumerate(POOL_WINDOWS):
            cols = slice(g * gdim, (g + 1) * gdim)
            s = hp_l[l][:, cols]
            for j in range(1, w):
                i = l - j
                s = s + (hp_l[i][:, cols] if i >= 0 else cpool_ref[pool_state + i][:, cols])
            cnt = float(min(w, PAST_LEN + l + 1))
            groups.append(s / cnt - hp_l[l][:, cols])
        pooled_rows.append(groups)
    pooled = [jnp.concatenate([pooled_rows[l][g] for l in range(n_new)], axis=0)
              for g in range(len(POOL_WINDOWS))]
    br_b = _dot(_pool_mix(pooled, poolw_ref, pscale_ref).astype(BF16), wb_ref[...])
    op_ref[...] = hp.reshape(n_new, nb, pool_dim)

    q = _dot(xb, wq_ref[...])
    kv = _dot(xb, wkv_ref[...])
    k_new = kv[:, :LANES]
    v_new = kv[:, LANES:]
    ok_ref[...] = k_new.reshape(n_new, nb, LANES)
    ov_ref[...] = v_new.reshape(n_new, nb, LANES)
    n_cached = nb * win_cache
    pad = jnp.zeros((LANES - rows, LANES), BF16)
    k_all = jnp.concatenate([ck_ref[...].reshape(n_cached, LANES).astype(BF16), k_new.astype(BF16), pad], axis=0)
    v_all = jnp.concatenate([cv_ref[...].reshape(n_cached, LANES).astype(BF16), v_new.astype(BF16), pad], axis=0)
    n_keys = n_cached + LANES
    r = lax.broadcasted_iota(jnp.int32, (rows, n_keys), 0)
    col = lax.broadcasted_iota(jnp.int32, (rows, n_keys), 1)
    q_l = r // nb
    q_b = r % nb
    cached = col < n_cached
    cn = col - n_cached
    k_b = jnp.where(cached, col // win_cache, cn % nb)
    dist_i = jnp.where(cached, q_l + win_cache - col % win_cache, q_l - cn // nb)
    valid = (k_b == q_b) & (dist_i >= 0) & (dist_i <= WINDOW) & (cached | (cn < rows))
    dist = dist_i.astype(F32)
    sinks = [sinks_ref[h] for h in range(N_HEADS)]
    cols = []
    for kvh in range(N_KV_HEADS):
        heads = [kvh * GROUP + a for a in range(GROUP)]
        q_pairs = [q[:, (kvh * GROUP + 2 * p) * HEAD_DIM:(kvh * GROUP + 2 * p + 2) * HEAD_DIM]
                   for p in range(GROUP // 2)]
        cols += _attention_group(q_pairs, _dup_lanes(k_all, kvh), _dup_lanes(v_all, kvh),
                                 lambda h: _slope(h) * dist, valid, sinks, heads)
    att = jnp.concatenate(cols, axis=-1)
    br_c = _dot(att.astype(BF16), wc_ref[...])

    y = _merge_and_norm(x, xb, br_a, br_b, br_c, wg_ref, wout_ref, ln1g_ref, ln1b_ref, alpha)
    y_ref[...] = y.reshape(n_new, nb, d_model)


def _sample_mixer(x_lb, cconv_t, cpool_t, ck, cv, sinks, wts, alpha, nb):
    n_new, n_seq, d_model = x_lb.shape
    conv_state, _, conv_dim = cconv_t.shape
    pool_state, _, pool_dim = cpool_t.shape
    win_cache = ck.shape[1]
    assert n_seq % nb == 0 and nb % SUBLANES == 0 and n_new * nb <= LANES
    names = ['w_cv', 'w_pl', 'w_q', 'w_kv', 'w_g', 'conv_w', 'conv_b', 'conv_ln_g', 'conv_ln_b', 'w_a',
             'pool_w', 'pool_scale', 'w_b', 'w_c', 'w_out', 'ln1_g', 'ln1_b']
    consts = [wts[n] for n in names]

    def lb_spec(rows, width):
        return pl.BlockSpec((rows, nb, width), lambda c: (0, c, 0))

    out_shapes = (
        jax.ShapeDtypeStruct((n_new, n_seq, d_model), F32),
        jax.ShapeDtypeStruct((n_new, n_seq, conv_dim), F32),
        jax.ShapeDtypeStruct((n_new, n_seq, pool_dim), F32),
        jax.ShapeDtypeStruct((n_new, n_seq, LANES), F32),
        jax.ShapeDtypeStruct((n_new, n_seq, LANES), F32),
    )
    return pl.pallas_call(
        functools.partial(_sample_mixer_kernel, alpha=alpha),
        grid=(n_seq // nb,),
        in_specs=[pl.BlockSpec(memory_space=pltpu.SMEM),
                  lb_spec(n_new, d_model), lb_spec(conv_state, conv_dim), lb_spec(pool_state, pool_dim),
                  pl.BlockSpec((nb, win_cache, LANES), lambda c: (c, 0, 0)),
                  pl.BlockSpec((nb, win_cache, LANES), lambda c: (c, 0, 0))]
                 + [_const_spec(a) for a in consts],
        out_specs=(lb_spec(n_new, d_model), lb_spec(n_new, conv_dim), lb_spec(n_new, pool_dim),
                   lb_spec(n_new, LANES), lb_spec(n_new, LANES)),
        out_shape=out_shapes,
        compiler_params=pltpu.CompilerParams(
            dimension_semantics=("arbitrary",), vmem_limit_bytes=VMEM_LIMIT_BYTES),
        name="sample_mixer",
    )(sinks, x_lb, cconv_t, cpool_t, ck, cv, *consts)


def _route(x, wr_hi_ref, wr_lo_ref, rbias_ref):
    n_experts = wr_hi_ref.shape[0]
    per_group = n_experts // N_EXPERT_GROUPS
    tm = x.shape[0]
    x_hi = x.astype(BF16)
    x_lo = (x - x_hi.astype(F32)).astype(BF16)
    logits = (_dot_nt(wr_hi_ref[...], x_hi) + _dot_nt(wr_lo_ref[...], x_hi)
              + _dot_nt(wr_hi_ref[...], x_lo))
    scores = jax.nn.sigmoid(logits)
    sel = scores + rbias_ref[...]
    sub = lax.broadcasted_iota(jnp.int32, (per_group, tm), 0)
    grp_scores = []
    for g in range(N_EXPERT_GROUPS):
        v = sel[g * per_group:(g + 1) * per_group]
        m1 = jnp.max(v, axis=0, keepdims=True)
        first = jnp.min(jnp.where(v == m1, sub, per_group), axis=0, keepdims=True)
        m2 = jnp.max(jnp.where(sub == first, -jnp.inf, v), axis=0, keepdims=True)
        grp_scores.append(m1 + m2)
    keep = []
    for g in range(N_EXPERT_GROUPS):
        rank = jnp.zeros((1, tm), jnp.int32)
        for o in range(N_EXPERT_GROUPS):
            if o == g:
                continue
            ahead = (grp_scores[o] >= grp_scores[g]) if o < g else (grp_scores[o] > grp_scores[g])
            rank = rank + ahead.astype(jnp.int32)
        keep.append(jnp.broadcast_to(rank, (per_group, tm)))
    cand = jnp.where(jnp.concatenate(keep, axis=0) < TOPK_GROUPS, sel, -jnp.inf)
    eidx = lax.broadcasted_iota(jnp.int32, (n_experts, tm), 0)
    chosen = jnp.zeros((n_experts, tm), F32)
    picks, weights = [], []
    for _ in range(TOP_K):
        m = jnp.max(cand, axis=0, keepdims=True)
        first = jnp.min(jnp.where(cand == m, eidx, n_experts), axis=0, keepdims=True)
        pick = eidx == first
        picks.append(first)
        weights.append(jnp.sum(jnp.where(pick, scores, 0.0), axis=0, keepdims=True))
        chosen = jnp.where(pick, 1.0, chosen)
        cand = jnp.where(pick, -jnp.inf, cand)
    total = weights[0]
    for w in weights[1:]:
        total = total + w
    gates = [w / total * ROUTED_SCALE for w in weights]
    return picks, gates, chosen


def _plan_kernel(x_ref, wrh_ref, wrl_ref, rbias_ref, gates_ref, pos_ref, cnt_ref, off_ref):
    n_experts = wrh_ref.shape[0]
    tm = x_ref.shape[0]
    picks, gates, chosen = _route(x_ref[...], wrh_ref, wrl_ref, rbias_ref)
    chosen_b = chosen.astype(BF16)
    before = (lax.broadcasted_iota(jnp.int32, (tm, tm), 0)
              < lax.broadcasted_iota(jnp.int32, (tm, tm), 1)).astype(BF16)
    rank_in_expert = _dot(chosen_b, before)
    lower = (lax.broadcasted_iota(jnp.int32, (n_experts, n_experts), 1)
             < lax.broadcasted_iota(jnp.int32, (n_experts, n_experts), 0)).astype(BF16)
    off = jnp.sum(_dot(lower, chosen_b), axis=1, keepdims=True)
    cnt = jnp.sum(chosen, axis=1, keepdims=True)
    rank = off + rank_in_expert
    eidx = lax.broadcasted_iota(jnp.int32, (n_experts, tm), 0)
    pos = [jnp.sum(jnp.where(eidx == p, rank, 0.0), axis=0, keepdims=True) for p in picks]
    gates_ref[0] = jnp.concatenate(gates, axis=0)
    pos_ref[0] = (jnp.concatenate(pos, axis=0) * SUBLANES).astype(jnp.int32)
    cnt_ref[0] = jnp.broadcast_to(cnt, (n_experts, LANES)).astype(jnp.int32)
    off_ref[0] = jnp.broadcast_to(off, (n_experts, LANES)).astype(jnp.int32)


def _moe_plan(x, wts, chunk):
    n_tok, d_model = x.shape
    n_experts = wts['wr_hi'].shape[0]
    n_chunks = n_tok // chunk
    consts = [wts['wr_hi'], wts['wr_lo'], wts['router_bias']]
    gates, pos, cnt, off = pl.pallas_call(
        _plan_kernel,
        grid=(n_chunks,),
        in_specs=[pl.BlockSpec((chunk, d_model), lambda c: (c, 0))] + [_const_spec(a) for a in consts],
        out_specs=(pl.BlockSpec((1, TOP_K, chunk), lambda c: (c, 0, 0)),
                   pl.BlockSpec((1, TOP_K, chunk), lambda c: (c, 0, 0)),
                   pl.BlockSpec((1, n_experts, LANES), lambda c: (c, 0, 0)),
                   pl.BlockSpec((1, n_experts, LANES), lambda c: (c, 0, 0))),
        out_shape=(jax.ShapeDtypeStruct((n_chunks, TOP_K, chunk), F32),
                   jax.ShapeDtypeStruct((n_chunks, TOP_K, chunk), jnp.int32),
                   jax.ShapeDtypeStruct((n_chunks, n_experts, LANES), jnp.int32),
                   jax.ShapeDtypeStruct((n_chunks, n_experts, LANES), jnp.int32)),
        compiler_params=pltpu.CompilerParams(
            dimension_semantics=("arbitrary",), vmem_limit_bytes=VMEM_LIMIT_BYTES),
        name="moe_plan",
    )(x, *consts)
    seg = jnp.concatenate([cnt[:, :, 0], off[:, :, 0]], axis=1)
    return gates.reshape(-1), pos.reshape(-1), seg.reshape(-1)


def _expert_kernel(pos_ref, gates_ref, seg_ref, x_ref, weg_ref, weu_ref, wed_ref,
                   wsg_ref, wsu_ref, wsd_ref, g_ref, b_ref, y_ref, sbuf, tbuf,
                   *, row_tile, experts_per_step, finish_rows, alpha):
    step = pl.program_id(1)
    tokens, d_model = x_ref.shape
    n_experts = seg_ref.shape[0] // 2
    pairs = tokens * TOP_K
    n_blocks = tokens // finish_rows

    def tile_at(ref, row):
        return ref.at[pl.ds(pl.multiple_of(row, SUBLANES), SUBLANES), :]

    def token_slabs(ref, first_row, rows):
        return [ref.at[pl.ds(first_row + j, rows, stride=SUBLANES), :] for j in range(SUBLANES)]

    @pl.when(step == 0)
    def _():
        sbuf[pairs * SUBLANES:, :] = jnp.zeros((row_tile * SUBLANES, LANES), F32)

        def spread(r, carry):
            rows = pl.ds(pl.multiple_of(r * finish_rows, finish_rows), finish_rows)
            base = pl.multiple_of(r * finish_rows * SUBLANES, SUBLANES)
            for j, slab in enumerate(token_slabs(tbuf, base, finish_rows)):
                slab[...] = x_ref[rows, j * LANES:(j + 1) * LANES]
            return carry
        lax.fori_loop(0, n_blocks, spread, 0)

        def scatter(i, carry):
            for u in range(SCATTER_UNROLL):
                t = i * SCATTER_UNROLL + u
                row = tile_at(tbuf, t * SUBLANES)[...]
                for k in range(TOP_K):
                    tile_at(sbuf, pos_ref[k * tokens + t])[...] = row
            return carry
        lax.fori_loop(0, tokens // SCATTER_UNROLL, scatter, 0)

    def hidden(lhs, j):
        lb = lhs.astype(BF16)
        return (jax.nn.silu(_dot(lb, weg_ref[j])) * _dot(lb, weu_ref[j])).astype(BF16)

    def expert_rows(lhs, j):
        return _dot(hidden(lhs, j), wed_ref[j])

    def load_tile(base):
        return jnp.concatenate([s[...] for s in token_slabs(sbuf, base, row_tile)], axis=-1)

    def store_tile(base, lhs, y, n_live):
        live = lax.broadcasted_iota(jnp.int32, lhs.shape, 0) < n_live
        y = jnp.where(live, y, lhs)
        for j, slab in enumerate(token_slabs(sbuf, base, row_tile)):
            slab[...] = y[:, j * LANES:(j + 1) * LANES]

    experts = [step * experts_per_step + j for j in range(experts_per_step)]
    n_rows = [seg_ref[ex] for ex in experts]
    bases = [pl.multiple_of(seg_ref[n_experts + ex] * SUBLANES, SUBLANES) for ex in experts]
    lhs = [load_tile(b) for b in bases]
    hs = [hidden(l, j) for j, l in enumerate(lhs)]
    ys = [_dot(h, wed_ref[j]) for j, h in enumerate(hs)]
    for j in range(experts_per_step):
        store_tile(bases[j], lhs[j], ys[j], n_rows[j])

    for j in range(experts_per_step):
        def run_tile(i, carry, j=j):
            base = pl.multiple_of(bases[j] + i * (row_tile * SUBLANES), SUBLANES)
            tile = load_tile(base)
            store_tile(base, tile, expert_rows(tile, j), n_rows[j] - i * row_tile)
            return carry
        lax.fori_loop(1, (n_rows[j] + row_tile - 1) // row_tile, run_tile, 0)

    @pl.when(step == pl.num_programs(1) - 1)
    def _():
        def combine(i, carry):
            for u in range(COMBINE_UNROLL):
                t = i * COMBINE_UNROLL + u
                acc = jnp.zeros((SUBLANES, LANES), F32)
                for k in range(TOP_K):
                    acc = acc + gates_ref[k * tokens + t] * tile_at(sbuf, pos_ref[k * tokens + t])[...]
                tile_at(tbuf, t * SUBLANES)[...] = acc
            return carry
        lax.fori_loop(0, tokens // COMBINE_UNROLL, combine, 0)

        def finish(r, carry):
            rows = pl.ds(pl.multiple_of(r * finish_rows, finish_rows), finish_rows)
            base = pl.multiple_of(r * finish_rows * SUBLANES, SUBLANES)
            routed = jnp.concatenate([s[...] for s in token_slabs(tbuf, base, finish_rows)], axis=-1)
            x = x_ref[rows, :]
            xb = x.astype(BF16)
            hs = jax.nn.silu(_dot(xb, wsg_ref[...])) * _dot(xb, wsu_ref[...])
            shared = _dot(hs.astype(BF16), wsd_ref[...])
            y_ref[rows, :] = _layer_norm(alpha * x + (routed + shared), g_ref[...], b_ref[...])
            return carry
        lax.fori_loop(0, n_blocks, finish, 0)


def _expert_row_tile(chunk, n_experts):
    mean = chunk * TOP_K / n_experts
    rows = mean + 3.0 * math.sqrt(mean * (1.0 - TOP_K / n_experts))
    return int(-(-rows // BF16_ROWS) * BF16_ROWS)


def _moe(x, wts, alpha, chunk):
    n_tok, d_model = x.shape
    n_experts, _, expert_dim = wts['w_e_gate'].shape
    assert d_model == SUBLANES * LANES and n_tok % chunk == 0 and n_experts % EXPERTS_PER_STEP == 0
    gates, pos, seg = _moe_plan(x, wts, chunk)
    row_tile = _expert_row_tile(chunk, n_experts)
    finish_rows = _pick_tile(chunk, 256)
    consts = [wts['w_s_gate'], wts['w_s_up'], wts['w_s_down'], wts['ln2_g'], wts['ln2_b']]
    once = pl.Buffered(1)
    kern = functools.partial(_expert_kernel, row_tile=row_tile, experts_per_step=EXPERTS_PER_STEP,
                             finish_rows=finish_rows, alpha=alpha)

    def w_spec(rows, cols):
        return pl.BlockSpec((EXPERTS_PER_STEP, rows, cols), lambda c, s: (s, 0, 0))

    return pl.pallas_call(
        kern,
        grid=(n_tok // chunk, n_experts // EXPERTS_PER_STEP),
        in_specs=[pl.BlockSpec((TOP_K * chunk,), lambda c, s: (c,), memory_space=pltpu.SMEM),
                  pl.BlockSpec((TOP_K * chunk,), lambda c, s: (c,), memory_space=pltpu.SMEM),
                  pl.BlockSpec((2 * n_experts,), lambda c, s: (c,), memory_space=pltpu.SMEM),
                  pl.BlockSpec((chunk, d_model), lambda c, s: (c, 0), pipeline_mode=once),
                  w_spec(d_model, expert_dim), w_spec(d_model, expert_dim), w_spec(expert_dim, d_model)]
                 + [pl.BlockSpec(a.shape, lambda c, s: (0, 0), pipeline_mode=once) for a in consts],
        out_specs=pl.BlockSpec((chunk, d_model), lambda c, s: (c, 0), pipeline_mode=once),
        out_shape=jax.ShapeDtypeStruct((n_tok, d_model), F32),
        scratch_shapes=[pltpu.VMEM(((chunk * TOP_K + row_tile) * SUBLANES, LANES), F32),
                        pltpu.VMEM((chunk * SUBLANES, LANES), F32)],
        compiler_params=pltpu.CompilerParams(
            dimension_semantics=("arbitrary", "arbitrary"), vmem_limit_bytes=EXPERT_VMEM_LIMIT_BYTES),
        name="moe_experts",
    )(pos, gates, seg, x, wts['w_e_gate'], wts['w_e_up'], wts['w_e_down'], *consts)


def _layer_weights(l, w_in, conv_w, conv_b, conv_ln_g, conv_ln_b, w_a, pool_w, pool_scale, w_b, w_c, w_out,
                   ln1_g, ln1_b, w_router, router_bias, w_e_gate, w_e_up, w_e_down, w_s_gate, w_s_up,
                   w_s_down, ln2_g, ln2_b):
    conv_dim = conv_w.shape[-1]
    pool_dim = pool_scale.shape[-1]
    attn_dim = w_c.shape[1]
    kv_dim = N_KV_HEADS * HEAD_DIM
    o1 = 2 * conv_dim
    o2 = o1 + pool_dim
    o3 = o2 + attn_dim
    o5 = o3 + 2 * kv_dim
    wi = w_in[l]
    row = lambda v: v[l][None, :].astype(F32)
    wr_t = w_router[l].T
    wr_hi = wr_t.astype(BF16)
    return {
        'w_cv': wi[:, :o1].astype(BF16), 'w_pl': wi[:, o1:o2].astype(BF16), 'w_q': wi[:, o2:o3].astype(BF16),
        'w_kv': wi[:, o3:o5].astype(BF16), 'w_g': wi[:, o5:].astype(BF16),
        'conv_w': conv_w[l], 'conv_b': row(conv_b), 'conv_ln_g': row(conv_ln_g), 'conv_ln_b': row(conv_ln_b),
        'w_a': w_a[l].astype(BF16), 'pool_w': pool_w[l].astype(BF16), 'pool_scale': row(pool_scale),
        'w_b': w_b[l].astype(BF16), 'w_c': w_c[l].astype(BF16), 'w_out': w_out[l].astype(BF16),
        'ln1_g': row(ln1_g), 'ln1_b': row(ln1_b),
        'wr_hi': wr_hi, 'wr_lo': (wr_t - wr_hi.astype(F32)).astype(BF16),
        'router_bias': router_bias[l][:, None].astype(F32),
        'w_e_gate': w_e_gate[l].astype(BF16), 'w_e_up': w_e_up[l].astype(BF16),
        'w_e_down': w_e_down[l].astype(BF16),
        'w_s_gate': w_s_gate[l].astype(BF16), 'w_s_up': w_s_up[l].astype(BF16),
        'w_s_down': w_s_down[l].astype(BF16),
        'ln2_g': row(ln2_g), 'ln2_b': row(ln2_b),
    }


def _pick_tile(n, target):
    t = min(n, target)
    while n % t:
        t //= 2
    return t


def kernel(x_prompt, x_sample, cache_conv, cache_pool, cache_k, cache_v, w_in, conv_w, conv_b, conv_ln_g,
           conv_ln_b, w_a, pool_w, pool_scale, w_b, attn_sinks, w_c, w_out, ln1_g, ln1_b, w_router,
           router_bias, w_e_gate, w_e_up, w_e_down, w_s_gate, w_s_up, w_s_down, ln2_g, ln2_b):
    depth = w_in.shape[0]
    batch, seq, d_model = x_prompt.shape
    n_seq, n_new, _ = x_sample.shape
    win_cache = cache_k.shape[2]
    alpha = (2 * depth) ** 0.25
    tq = _pick_tile(seq, 512)
    nb = _pick_tile(n_seq, LANES // n_new // 2)
    tm_prompt = _pick_tile(batch * seq, 1024)
    tm_sample = _pick_tile(n_seq * n_new, 1024)

    yp = x_prompt
    ys = jnp.transpose(x_sample, (1, 0, 2))
    outs = [[] for _ in range(8)]
    for l in range(depth):
        wts = _layer_weights(l, w_in, conv_w, conv_b, conv_ln_g, conv_ln_b, w_a, pool_w, pool_scale, w_b,
                             w_c, w_out, ln1_g, ln1_b, w_router, router_bias, w_e_gate, w_e_up, w_e_down,
                             w_s_gate, w_s_up, w_s_down, ln2_g, ln2_b)
        sinks = attn_sinks[l].astype(F32)

        yp, c1, p1, k1, v1 = _prompt_mixer(yp, sinks, wts, alpha, tq)
        yp = _moe(yp.reshape(batch * seq, d_model), wts, alpha, tm_prompt).reshape(batch, seq, d_model)

        ck = cache_k[l].reshape(n_seq, win_cache, LANES)
        cv = cache_v[l].reshape(n_seq, win_cache, LANES)
        ys, u2, hp2, k2, v2 = _sample_mixer(
            ys, jnp.transpose(cache_conv[l], (1, 0, 2)), jnp.transpose(cache_pool[l], (1, 0, 2)),
            ck, cv, sinks, wts, alpha, nb)
        ys = _moe(ys.reshape(n_new * n_seq, d_model), wts, alpha, tm_sample).reshape(n_new, n_seq, d_model)

        to_bl = lambda a: jnp.transpose(a, (1, 0, 2))
        outs[0].append(c1)
        outs[1].append(p1)
        outs[2].append(k1.reshape(batch, WINDOW, N_KV_HEADS, HEAD_DIM))
        outs[3].append(v1.reshape(batch, WINDOW, N_KV_HEADS, HEAD_DIM))
        outs[4].append(jnp.concatenate([cache_conv[l], to_bl(u2)], axis=1)[:, n_new:])
        outs[5].append(jnp.concatenate([cache_pool[l], to_bl(hp2)], axis=1)[:, n_new:])
        outs[6].append(jnp.concatenate([ck, to_bl(k2)], axis=1)[:, n_new:]
                       .reshape(n_seq, win_cache, N_KV_HEADS, HEAD_DIM))
        outs[7].append(jnp.concatenate([cv, to_bl(v2)], axis=1)[:, n_new:]
                       .reshape(n_seq, win_cache, N_KV_HEADS, HEAD_DIM))
    return (yp, jnp.transpose(ys, (1, 0, 2))) + tuple(jnp.stack(o) for o in outs)
```

```python
import functools
import math

import jax
import jax.numpy as jnp
from jax import lax
from jax.experimental import pallas as pl
from jax.experimental.pallas import tpu as pltpu

PAST_LEN = 16384
WINDOW = 128
BLOCK = 128
N_HEADS = 8
N_KV_HEADS = 2
GROUP = N_HEADS // N_KV_HEADS
HEAD_DIM = 64
POOL_WINDOWS = (2, 4, 8, 16)
N_EXPERT_GROUPS = 8
TOPK_GROUPS = 4
TOP_K = 8
ROUTED_SCALE = 2.5
LN_EPS = 1e-5
NEG_BIG = -1e30

LANES = 128
SUBLANES = 8
VMEM_LIMIT_BYTES = 56 * 1024 * 1024
EXPERT_VMEM_LIMIT_BYTES = 60 * 1024 * 1024
BF16_ROWS = 16
SLOT_ROWS = 4
PLAN_SUB_TOKENS = 1024
MOE_CHUNK_TOKENS = 2048
EXPERTS_PER_STEP = 1
SCATTER_UNROLL = 8
COMBINE_UNROLL = 4

BF16 = jnp.bfloat16
F32 = jnp.float32


def _dot(a, b):
    return jnp.dot(a, b, preferred_element_type=F32)


def _dot_nt(a, b):
    return lax.dot_general(a, b, (((1,), (1,)), ((), ())), preferred_element_type=F32)


def _layer_norm(x, g, b):
    mu = jnp.mean(x, axis=-1, keepdims=True)
    xc = x - mu
    var = jnp.mean(xc * xc, axis=-1, keepdims=True)
    return xc * lax.rsqrt(var + LN_EPS) * g + b


def _slope(head):
    return 2.0 ** (-8.0 * (head + 1) / N_HEADS)


def _dup_lanes(x, half):
    rolled = pltpu.roll(x, HEAD_DIM, axis=1)
    lane = lax.broadcasted_iota(jnp.int32, x.shape, 1)
    lo = lane < HEAD_DIM
    if half == 0:
        return jnp.where(lo, x, rolled)
    return jnp.where(lo, rolled, x)


def _attention_group(q_pairs, kk, vv, bias_fn, valid, sinks, heads):
    rows = q_pairs[0].shape[0]
    lane = lax.broadcasted_iota(jnp.int32, (rows, LANES), 1)
    lo = lane < HEAD_DIM
    stacked = []
    for qp in q_pairs:
        stacked.append(jnp.where(lo, qp, 0.0).astype(BF16))
        stacked.append(jnp.where(lo, 0.0, qp).astype(BF16))
    lhs = jnp.concatenate(stacked, axis=0)
    s_all = _dot_nt(lhs, kk)
    probs = []
    for a, head in enumerate(heads):
        s = s_all[a * rows:(a + 1) * rows] * (HEAD_DIM ** -0.5) - bias_fn(head)
        s = jnp.where(valid, s, NEG_BIG)
        sink = sinks[head]
        m = jnp.maximum(jnp.max(s, axis=-1, keepdims=True), sink)
        e = jnp.exp(s - m)
        denom = jnp.sum(e, axis=-1, keepdims=True) + jnp.exp(sink - m)
        probs.append((e / denom).astype(BF16))
    o_all = _dot(jnp.concatenate(probs, axis=0), vv)
    outs = []
    for pair in range(GROUP // 2):
        o_lo = o_all[(2 * pair) * rows:(2 * pair + 1) * rows]
        o_hi = o_all[(2 * pair + 1) * rows:(2 * pair + 2) * rows]
        outs.append(jnp.where(lo, o_lo, o_hi))
    return outs


def _pool_mix(pooled_groups, poolw_ref, pscale_ref):
    mixed = [_dot(p.astype(BF16), poolw_ref[g]) for g, p in enumerate(pooled_groups)]
    return jnp.concatenate(mixed, axis=-1) * pscale_ref[...]


def _merge_and_norm(x, xb, br_a, br_b, br_c, wg_ref, wout_ref, g_ref, b_ref, alpha):
    d_model = x.shape[-1]
    gates = jax.nn.sigmoid(_dot(xb, wg_ref[...]))
    merged = (gates[:, :d_model] * br_a + gates[:, d_model:2 * d_model] * br_b
              + gates[:, 2 * d_model:] * br_c)
    mix = _dot(merged.astype(BF16), wout_ref[...])
    return _layer_norm(alpha * x + mix, g_ref[...], b_ref[...])


def _prompt_mixer_kernel(sinks_ref, x_ref, wcv_ref, wpl_ref, wq_ref, wkv_ref, wg_ref,
                         convw_ref, convb_ref, clng_ref, clnb_ref, wa_ref,
                         poolw_ref, pscale_ref, wb_ref, wc_ref, wout_ref, ln1g_ref, ln1b_ref,
                         y_ref, oconv_ref, opool_ref, ok_ref, ov_ref,
                         ucat, pcat, kcat, vcat, *, tq, conv_hist, pool_hist, alpha):
    t = pl.program_id(1)
    conv_width = convw_ref.shape[0]
    conv_dim = convw_ref.shape[1]
    pool_dim = wpl_ref.shape[1]
    gdim = pool_dim // len(POOL_WINDOWS)
    conv_state = conv_width - 1
    pool_state = max(POOL_WINDOWS) - 1

    @pl.when(t == 0)
    def _():
        ucat[0:conv_hist, :] = jnp.zeros((conv_hist, conv_dim), F32)
        pcat[0:pool_hist, :] = jnp.zeros((pool_hist, pool_dim), F32)
        kcat[0:BLOCK, :] = jnp.zeros((BLOCK, LANES), F32)
        vcat[0:BLOCK, :] = jnp.zeros((BLOCK, LANES), F32)

    x = x_ref[0]
    xb = x.astype(BF16)

    hc = _dot(xb, wcv_ref[...])
    u = hc[:, :conv_dim] * jax.nn.sigmoid(hc[:, conv_dim:])
    ucat[conv_hist:conv_hist + tq, :] = u
    acc = jnp.zeros((tq, conv_dim), F32) + convb_ref[...]
    base = conv_hist - conv_state
    for j in range(conv_width):
        acc = acc + convw_ref[j:j + 1, :] * ucat[base + j:base + j + tq, :]
    c = _layer_norm(acc, clng_ref[...], clnb_ref[...])
    c = c * jax.nn.sigmoid(c)
    br_a = _dot(c.astype(BF16), wa_ref[...])
    oconv_ref[0] = ucat[conv_hist + tq - conv_state:conv_hist + tq, :]
    ucat[0:conv_hist, :] = ucat[tq:tq + conv_hist, :]

    hp = _dot(xb, wpl_ref[...])
    pcat[pool_hist:pool_hist + tq, :] = hp
    pos = t * tq + lax.broadcasted_iota(jnp.int32, (tq, gdim), 0)
    pooled = []
    for g, w in enumerate(POOL_WINDOWS):
        cols = slice(g * gdim, (g + 1) * gdim)
        s = pcat[pool_hist:pool_hist + tq, cols]
        cur = s
        for j in range(1, w):
            s = s + pcat[pool_hist - j:pool_hist - j + tq, cols]
        cnt = jnp.minimum(w, pos + 1).astype(F32)
        pooled.append(s / cnt - cur)
    br_b = _dot(_pool_mix(pooled, poolw_ref, pscale_ref).astype(BF16), wb_ref[...])
    opool_ref[0] = pcat[pool_hist + tq - pool_state:pool_hist + tq, :]
    pcat[0:pool_hist, :] = pcat[tq:tq + pool_hist, :]

    q = _dot(xb, wq_ref[...])
    kv = _dot(xb, wkv_ref[...])
    kcat[BLOCK:BLOCK + tq, :] = kv[:, :LANES]
    vcat[BLOCK:BLOCK + tq, :] = kv[:, LANES:]
    sinks = [sinks_ref[h] for h in range(N_HEADS)]
    qi = lax.broadcasted_iota(jnp.int32, (BLOCK, 2 * BLOCK), 0)
    kj = lax.broadcasted_iota(jnp.int32, (BLOCK, 2 * BLOCK), 1)
    dist_i = qi + BLOCK - kj
    dist = dist_i.astype(F32)
    in_window = (dist_i >= 0) & (dist_i <= WINDOW)
    first_key = jnp.where(t == 0, BLOCK, 0)
    att_blocks = []
    for b in range(tq // BLOCK):
        valid = in_window & (kj >= first_key) if b == 0 else in_window
        kblk = kcat[b * BLOCK:(b + 2) * BLOCK, :].astype(BF16)
        vblk = vcat[b * BLOCK:(b + 2) * BLOCK, :].astype(BF16)
        qblk = q[b * BLOCK:(b + 1) * BLOCK]
        cols = []
        for kvh in range(N_KV_HEADS):
            heads = [kvh * GROUP + a for a in range(GROUP)]
            q_pairs = [qblk[:, (kvh * GROUP + 2 * p) * HEAD_DIM:(kvh * GROUP + 2 * p + 2) * HEAD_DIM]
                       for p in range(GROUP // 2)]
            cols += _attention_group(q_pairs, _dup_lanes(kblk, kvh), _dup_lanes(vblk, kvh),
                                     lambda h: _slope(h) * dist, valid, sinks, heads)
        att_blocks.append(jnp.concatenate(cols, axis=-1))
    att = jnp.concatenate(att_blocks, axis=0)
    br_c = _dot(att.astype(BF16), wc_ref[...])
    ok_ref[0] = kcat[tq:tq + BLOCK, :]
    ov_ref[0] = vcat[tq:tq + BLOCK, :]
    kcat[0:BLOCK, :] = kcat[tq:tq + BLOCK, :]
    vcat[0:BLOCK, :] = vcat[tq:tq + BLOCK, :]

    y_ref[0] = _merge_and_norm(x, xb, br_a, br_b, br_c, wg_ref, wout_ref, ln1g_ref, ln1b_ref, alpha)


def _const_spec(arr):
    nd = arr.ndim
    return pl.BlockSpec(arr.shape, lambda *_: (0,) * nd)


def _prompt_mixer(x, sinks, wts, alpha, tq):
    batch, seq, d_model = x.shape
    conv_width, conv_dim = wts['conv_w'].shape
    pool_dim = wts['w_pl'].shape[1]
    conv_state = conv_width - 1
    pool_state = max(POOL_WINDOWS) - 1
    conv_hist = -(-conv_state // SUBLANES) * SUBLANES
    pool_hist = -(-pool_state // SUBLANES) * SUBLANES
    assert seq % tq == 0 and tq % BLOCK == 0 and tq >= max(conv_hist, pool_hist, BLOCK)
    names = ['w_cv', 'w_pl', 'w_q', 'w_kv', 'w_g', 'conv_w', 'conv_b', 'conv_ln_g', 'conv_ln_b', 'w_a',
             'pool_w', 'pool_scale', 'w_b', 'w_c', 'w_out', 'ln1_g', 'ln1_b']
    consts = [wts[n] for n in names]
    kern = functools.partial(_prompt_mixer_kernel, tq=tq, conv_hist=conv_hist, pool_hist=pool_hist,
                             alpha=alpha)
    out_shapes = (
        jax.ShapeDtypeStruct((batch, seq, d_model), F32),
        jax.ShapeDtypeStruct((batch, conv_state, conv_dim), F32),
        jax.ShapeDtypeStruct((batch, pool_state, pool_dim), F32),
        jax.ShapeDtypeStruct((batch, WINDOW, LANES), F32),
        jax.ShapeDtypeStruct((batch, WINDOW, LANES), F32),
    )
    return pl.pallas_call(
        kern,
        grid=(batch, seq // tq),
        in_specs=[pl.BlockSpec(memory_space=pltpu.SMEM),
                  pl.BlockSpec((1, tq, d_model), lambda b, t: (b, t, 0))]
                 + [_const_spec(a) for a in consts],
        out_specs=(
            pl.BlockSpec((1, tq, d_model), lambda b, t: (b, t, 0)),
            pl.BlockSpec((1, conv_state, conv_dim), lambda b, t: (b, 0, 0)),
            pl.BlockSpec((1, pool_state, pool_dim), lambda b, t: (b, 0, 0)),
            pl.BlockSpec((1, WINDOW, LANES), lambda b, t: (b, 0, 0)),
            pl.BlockSpec((1, WINDOW, LANES), lambda b, t: (b, 0, 0)),
        ),
        out_shape=out_shapes,
        scratch_shapes=[
            pltpu.VMEM((conv_hist + tq, conv_dim), F32),
            pltpu.VMEM((pool_hist + tq, pool_dim), F32),
            pltpu.VMEM((BLOCK + tq, LANES), F32),
            pltpu.VMEM((BLOCK + tq, LANES), F32),
        ],
        compiler_params=pltpu.CompilerParams(
            dimension_semantics=("arbitrary", "arbitrary"), vmem_limit_bytes=VMEM_LIMIT_BYTES),
        name="prompt_mixer",
    )(sinks, x, *consts)


def _sample_mixer_kernel(sinks_ref, x_ref, cconv_ref, cpool_ref, ck_ref, cv_ref,
                         wcv_ref, wpl_ref, wq_ref, wkv_ref, wg_ref,
                         convw_ref, convb_ref, clng_ref, clnb_ref, wa_ref,
                         poolw_ref, pscale_ref, wb_ref, wc_ref, wout_ref, ln1g_ref, ln1b_ref,
                         y_ref, ou_ref, op_ref, ok_ref, ov_ref, *, alpha):
    n_new, nb, d_model = x_ref.shape
    conv_width, conv_dim = convw_ref.shape
    conv_state = conv_width - 1
    pool_dim = wpl_ref.shape[1]
    gdim = pool_dim // len(POOL_WINDOWS)
    pool_state = max(POOL_WINDOWS) - 1
    win_cache = ck_ref.shape[1]
    rows = n_new * nb

    x = x_ref[...].reshape(rows, d_model)
    xb = x.astype(BF16)

    hc = _dot(xb, wcv_ref[...])
    u = hc[:, :conv_dim] * jax.nn.sigmoid(hc[:, conv_dim:])
    u_l = [u[l * nb:(l + 1) * nb] for l in range(n_new)]
    conv_rows = []
    for l in range(n_new):
        acc = jnp.zeros((nb, conv_dim), F32) + convb_ref[...]
        for j in range(conv_width):
            i = l + j
            src = cconv_ref[i] if i < conv_state else u_l[i - conv_state]
            acc = acc + convw_ref[j:j + 1, :] * src
        conv_rows.append(acc)
    c = _layer_norm(jnp.concatenate(conv_rows, axis=0), clng_ref[...], clnb_ref[...])
    c = c * jax.nn.sigmoid(c)
    br_a = _dot(c.astype(BF16), wa_ref[...])
    ou_ref[...] = u.reshape(n_new, nb, conv_dim)

    hp = _dot(xb, wpl_ref[...])
    hp_l = [hp[l * nb:(l + 1) * nb] for l in range(n_new)]
    pooled_rows = []
    for l in range(n_new):
        groups = []
        for g, w in enumerate(POOL_WINDOWS):
            cols = slice(g * gdim, (g + 1) * gdim)
            s = hp_l[l][:, cols]
            for j in range(1, w):
                i = l - j
                s = s + (hp_l[i][:, cols] if i >= 0 else cpool_ref[pool_state + i][:, cols])
            cnt = float(min(w, PAST_LEN + l + 1))
            groups.append(s / cnt - hp_l[l][:, cols])
        pooled_rows.append(groups)
    pooled = [jnp.concatenate([pooled_rows[l][g] for l in range(n_new)], axis=0)
              for g in range(len(POOL_WINDOWS))]
    br_b = _dot(_pool_mix(pooled, poolw_ref, pscale_ref).astype(BF16), wb_ref[...])
    op_ref[...] = hp.reshape(n_new, nb, pool_dim)

    q = _dot(xb, wq_ref[...])
    kv = _dot(xb, wkv_ref[...])
    k_new = kv[:, :LANES]
    v_new = kv[:, LANES:]
    ok_ref[...] = k_new.reshape(n_new, nb, LANES)
    ov_ref[...] = v_new.reshape(n_new, nb, LANES)
    n_cached = nb * win_cache
    pad = jnp.zeros((LANES - rows, LANES), BF16)
    k_all = jnp.concatenate([ck_ref[...].reshape(n_cached, LANES).astype(BF16), k_new.astype(BF16), pad], axis=0)
    v_all = jnp.concatenate([cv_ref[...].reshape(n_cached, LANES).astype(BF16), v_new.astype(BF16), pad], axis=0)
    n_keys = n_cached + LANES
    r = lax.broadcasted_iota(jnp.int32, (rows, n_keys), 0)
    col = lax.broadcasted_iota(jnp.int32, (rows, n_keys), 1)
    q_l = r // nb
    q_b = r % nb
    cached = col < n_cached
    cn = col - n_cached
    k_b = jnp.where(cached, col // win_cache, cn % nb)
    dist_i = jnp.where(cached, q_l + win_cache - col % win_cache, q_l - cn // nb)
    valid = (k_b == q_b) & (dist_i >= 0) & (dist_i <= WINDOW) & (cached | (cn < rows))
    dist = dist_i.astype(F32)
    sinks = [sinks_ref[h] for h in range(N_HEADS)]
    cols = []
    for kvh in range(N_KV_HEADS):
        heads = [kvh * GROUP + a for a in range(GROUP)]
        q_pairs = [q[:, (kvh * GROUP + 2 * p) * HEAD_DIM:(kvh * GROUP + 2 * p + 2) * HEAD_DIM]
                   for p in range(GROUP // 2)]
        cols += _attention_group(q_pairs, _dup_lanes(k_all, kvh), _dup_lanes(v_all, kvh),
                                 lambda h: _slope(h) * dist, valid, sinks, heads)
    att = jnp.concatenate(cols, axis=-1)
    br_c = _dot(att.astype(BF16), wc_ref[...])

    y = _merge_and_norm(x, xb, br_a, br_b, br_c, wg_ref, wout_ref, ln1g_ref, ln1b_ref, alpha)
    y_ref[...] = y.reshape(n_new, nb, d_model)


def _sample_mixer(x_lb, cconv_t, cpool_t, ck, cv, sinks, wts, alpha, nb):
    n_new, n_seq, d_model = x_lb.shape
    conv_state, _, conv_dim = cconv_t.shape
    pool_state, _, pool_dim = cpool_t.shape
    win_cache = ck.shape[1]
    assert n_seq % nb == 0 and nb % SUBLANES == 0 and n_new * nb <= LANES
    names = ['w_cv', 'w_pl', 'w_q', 'w_kv', 'w_g', 'conv_w', 'conv_b', 'conv_ln_g', 'conv_ln_b', 'w_a',
             'pool_w', 'pool_scale', 'w_b', 'w_c', 'w_out', 'ln1_g', 'ln1_b']
    consts = [wts[n] for n in names]

    def lb_spec(rows, width):
        return pl.BlockSpec((rows, nb, width), lambda c: (0, c, 0))

    out_shapes = (
        jax.ShapeDtypeStruct((n_new, n_seq, d_model), F32),
        jax.ShapeDtypeStruct((n_new, n_seq, conv_dim), F32),
        jax.ShapeDtypeStruct((n_new, n_seq, pool_dim), F32),
        jax.ShapeDtypeStruct((n_new, n_seq, LANES), F32),
        jax.ShapeDtypeStruct((n_new, n_seq, LANES), F32),
    )
    return pl.pallas_call(
        functools.partial(_sample_mixer_kernel, alpha=alpha),
        grid=(n_seq // nb,),
        in_specs=[pl.BlockSpec(memory_space=pltpu.SMEM),
                  lb_spec(n_new, d_model), lb_spec(conv_state, conv_dim), lb_spec(pool_state, pool_dim),
                  pl.BlockSpec((nb, win_cache, LANES), lambda c: (c, 0, 0)),
                  pl.BlockSpec((nb, win_cache, LANES), lambda c: (c, 0, 0))]
                 + [_const_spec(a) for a in consts],
        out_specs=(lb_spec(n_new, d_model), lb_spec(n_new, conv_dim), lb_spec(n_new, pool_dim),
                   lb_spec(n_new, LANES), lb_spec(n_new, LANES)),
        out_shape=out_shapes,
        compiler_params=pltpu.CompilerParams(
            dimension_semantics=("arbitrary",), vmem_limit_bytes=VMEM_LIMIT_BYTES),
        name="sample_mixer",
    )(sinks, x_lb, cconv_t, cpool_t, ck, cv, *consts)


def _route(x, wr_hi_ref, wr_lo_ref, rbias_ref):
    n_experts = wr_hi_ref.shape[0]
    per_group = n_experts // N_EXPERT_GROUPS
    tm = x.shape[0]
    x_hi = x.astype(BF16)
    x_lo = (x - x_hi.astype(F32)).astype(BF16)
    logits = (_dot_nt(wr_hi_ref[...], x_hi) + _dot_nt(wr_lo_ref[...], x_hi)
              + _dot_nt(wr_hi_ref[...], x_lo))
    scores = jax.nn.sigmoid(logits)
    sel = scores + rbias_ref[...]
    sub = lax.broadcasted_iota(jnp.int32, (per_group, tm), 0)
    grp_scores = []
    for g in range(N_EXPERT_GROUPS):
        v = sel[g * per_group:(g + 1) * per_group]
        m1 = jnp.max(v, axis=0, keepdims=True)
        first = jnp.min(jnp.where(v == m1, sub, per_group), axis=0, keepdims=True)
        m2 = jnp.max(jnp.where(sub == first, -jnp.inf, v), axis=0, keepdims=True)
        grp_scores.append(m1 + m2)
    keep = []
    for g in range(N_EXPERT_GROUPS):
        rank = jnp.zeros((1, tm), jnp.int32)
        for o in range(N_EXPERT_GROUPS):
            if o == g:
                continue
            ahead = (grp_scores[o] >= grp_scores[g]) if o < g else (grp_scores[o] > grp_scores[g])
            rank = rank + ahead.astype(jnp.int32)
        keep.append(jnp.broadcast_to(rank, (per_group, tm)))
    cand = jnp.where(jnp.concatenate(keep, axis=0) < TOPK_GROUPS, sel, -jnp.inf)
    eidx = lax.broadcasted_iota(jnp.int32, (n_experts, tm), 0)
    chosen = jnp.zeros((n_experts, tm), F32)
    picks, weights = [], []
    for _ in range(TOP_K):
        m = jnp.max(cand, axis=0, keepdims=True)
        first = jnp.min(jnp.where(cand == m, eidx, n_experts), axis=0, keepdims=True)
        pick = eidx == first
        picks.append(first)
        weights.append(jnp.sum(jnp.where(pick, scores, 0.0), axis=0, keepdims=True))
        chosen = jnp.where(pick, 1.0, chosen)
        cand = jnp.where(pick, -jnp.inf, cand)
    total = weights[0]
    for w in weights[1:]:
        total = total + w
    gates = [w / total * ROUTED_SCALE for w in weights]
    return picks, gates, chosen


def _plan_kernel(x_ref, wrh_ref, wrl_ref, rbias_ref, gates_ref, pos_ref, cnt_ref, off_ref):
    n_experts = wrh_ref.shape[0]
    tm = x_ref.shape[0]
    sub = min(tm, PLAN_SUB_TOKENS)
    picks, gates, chosen = _route(x_ref[...], wrh_ref, wrl_ref, rbias_ref)
    chosen_b = chosen.astype(BF16)
    before = (lax.broadcasted_iota(jnp.int32, (sub, sub), 0)
              < lax.broadcasted_iota(jnp.int32, (sub, sub), 1)).astype(BF16)
    ranks = []
    cnt = jnp.zeros((n_experts, 1), F32)
    for b in range(tm // sub):
        ranks.append(_dot(chosen_b[:, b * sub:(b + 1) * sub], before) + cnt)
        cnt = cnt + jnp.sum(chosen[:, b * sub:(b + 1) * sub], axis=1, keepdims=True)
    rank_in_expert = jnp.concatenate(ranks, axis=1)
    lower = (lax.broadcasted_iota(jnp.int32, (n_experts, n_experts), 1)
             < lax.broadcasted_iota(jnp.int32, (n_experts, n_experts), 0)).astype(BF16)
    off = jnp.sum(_dot(lower, chosen_b), axis=1, keepdims=True)
    rank = off + rank_in_expert
    eidx = lax.broadcasted_iota(jnp.int32, (n_experts, tm), 0)
    pos = [jnp.sum(jnp.where(eidx == p, rank, 0.0), axis=0, keepdims=True) for p in picks]
    gates_ref[0] = jnp.concatenate(gates, axis=0)
    pos_ref[0] = (jnp.concatenate(pos, axis=0) * SLOT_ROWS).astype(jnp.int32)
    cnt_ref[0] = jnp.broadcast_to(cnt, (n_experts, LANES)).astype(jnp.int32)
    off_ref[0] = jnp.broadcast_to(off, (n_experts, LANES)).astype(jnp.int32)


def _moe_plan(x, wts, chunk):
    n_tok, d_model = x.shape
    n_experts = wts['wr_hi'].shape[0]
    n_chunks = n_tok // chunk
    consts = [wts['wr_hi'], wts['wr_lo'], wts['router_bias']]
    gates, pos, cnt, off = pl.pallas_call(
        _plan_kernel,
        grid=(n_chunks,),
        in_specs=[pl.BlockSpec((chunk, d_model), lambda c: (c, 0))] + [_const_spec(a) for a in consts],
        out_specs=(pl.BlockSpec((1, TOP_K, chunk), lambda c: (c, 0, 0)),
                   pl.BlockSpec((1, TOP_K, chunk), lambda c: (c, 0, 0)),
                   pl.BlockSpec((1, n_experts, LANES), lambda c: (c, 0, 0)),
                   pl.BlockSpec((1, n_experts, LANES), lambda c: (c, 0, 0))),
        out_shape=(jax.ShapeDtypeStruct((n_chunks, TOP_K, chunk), F32),
                   jax.ShapeDtypeStruct((n_chunks, TOP_K, chunk), jnp.int32),
                   jax.ShapeDtypeStruct((n_chunks, n_experts, LANES), jnp.int32),
                   jax.ShapeDtypeStruct((n_chunks, n_experts, LANES), jnp.int32)),
        compiler_params=pltpu.CompilerParams(
            dimension_semantics=("arbitrary",), vmem_limit_bytes=VMEM_LIMIT_BYTES),
        name="moe_plan",
    )(x, *consts)
    seg = jnp.concatenate([cnt[:, :, 0], off[:, :, 0]], axis=1)
    per_pick = lambda a: [a[:, k, :].reshape(-1) for k in range(TOP_K)]
    return per_pick(gates), per_pick(pos), seg.reshape(-1)


def _expert_kernel(*refs, row_tile, experts_per_step, block_rows, alpha):
    pos_refs, gate_refs = refs[:TOP_K], refs[TOP_K:2 * TOP_K]
    (seg_ref, x_ref, weg_ref, weu_ref, wed_ref, wsg_ref, wsu_ref, wsd_ref, g_ref, b_ref,
     y_ref, sbuf, pbuf, top_buf, bot_buf) = refs[2 * TOP_K:]
    step = pl.program_id(1)
    tokens, d_model = x_ref.shape
    half = d_model // 2
    n_experts = seg_ref.shape[0] // 2
    pairs = tokens * TOP_K
    n_blocks = tokens // block_rows

    def pack(a, b):
        return pltpu.pack_elementwise([a, b], packed_dtype=BF16)

    def unpack(w):
        return tuple(pltpu.unpack_elementwise(w, index=i, packed_dtype=BF16, unpacked_dtype=F32)
                     for i in range(2))

    def slot_at(ref, row):
        return ref.at[pl.ds(pl.multiple_of(row, SLOT_ROWS), SLOT_ROWS), :]

    def slabs(ref, first_row, rows):
        return [ref.at[pl.ds(first_row + j, rows, stride=SLOT_ROWS), :] for j in range(SLOT_ROWS)]

    @pl.when(step == 0)
    def _():
        sbuf[pairs * SLOT_ROWS:, :] = jnp.zeros((row_tile * SLOT_ROWS, LANES), jnp.uint32)

        def scatter_block(r, carry):
            rows = pl.ds(pl.multiple_of(r * block_rows, block_rows), block_rows)
            words = pack(x_ref[rows, :half], x_ref[rows, half:])
            for j, slab in enumerate(slabs(pbuf, 0, block_rows)):
                slab[...] = words[:, j * LANES:(j + 1) * LANES]

            def scatter(i, carry):
                for u in range(SCATTER_UNROLL):
                    local = i * SCATTER_UNROLL + u
                    t = r * block_rows + local
                    slot = slot_at(pbuf, local * SLOT_ROWS)[...]
                    for k in range(TOP_K):
                        slot_at(sbuf, pos_refs[k][t])[...] = slot
                return carry
            return lax.fori_loop(0, block_rows // SCATTER_UNROLL, scatter, carry)
        lax.fori_loop(0, n_blocks, scatter_block, 0)

    def unpack_rows(words):
        return jnp.concatenate(unpack(words), axis=-1)

    def hidden(words, j):
        lb = unpack_rows(words).astype(BF16)
        return (jax.nn.silu(_dot(lb, weg_ref[j])) * _dot(lb, weu_ref[j])).astype(BF16)

    def expert_rows(words, j):
        return _dot(hidden(words, j), wed_ref[j])

    def load_tile(base):
        return jnp.concatenate([s[...] for s in slabs(sbuf, base, row_tile)], axis=-1)

    def store_tile(base, words, y, n_live):
        live = lax.broadcasted_iota(jnp.int32, y.shape, 0) < n_live
        out = jnp.where(live, y, unpack_rows(words))
        new_words = pack(out[:, :half], out[:, half:])
        for j, slab in enumerate(slabs(sbuf, base, row_tile)):
            slab[...] = new_words[:, j * LANES:(j + 1) * LANES]

    experts = [step * experts_per_step + j for j in range(experts_per_step)]
    n_rows = [seg_ref[ex] for ex in experts]
    bases = [pl.multiple_of(seg_ref[n_experts + ex] * SLOT_ROWS, SLOT_ROWS) for ex in experts]
    lhs = [load_tile(b) for b in bases]
    hs = [hidden(l, j) for j, l in enumerate(lhs)]
    ys = [_dot(h, wed_ref[j]) for j, h in enumerate(hs)]
    for j in range(experts_per_step):
        store_tile(bases[j], lhs[j], ys[j], n_rows[j])

    for j in range(experts_per_step):
        def run_tile(i, carry, j=j):
            base = pl.multiple_of(bases[j] + i * (row_tile * SLOT_ROWS), SLOT_ROWS)
            tile = load_tile(base)
            store_tile(base, tile, expert_rows(tile, j), n_rows[j] - i * row_tile)
            return carry
        lax.fori_loop(1, (n_rows[j] + row_tile - 1) // row_tile, run_tile, 0)

    @pl.when(step == pl.num_programs(1) - 1)
    def _():
        def finish(r, carry):
            def combine(i, carry):
                for u in range(COMBINE_UNROLL):
                    local = i * COMBINE_UNROLL + u
                    t = r * block_rows + local
                    top = jnp.zeros((SLOT_ROWS, LANES), F32)
                    bot = jnp.zeros((SLOT_ROWS, LANES), F32)
                    for k in range(TOP_K):
                        a, b = unpack(slot_at(sbuf, pos_refs[k][t])[...])
                        gate = gate_refs[k][t]
                        top = top + gate * a
                        bot = bot + gate * b
                    slot_at(top_buf, local * SLOT_ROWS)[...] = top
                    slot_at(bot_buf, local * SLOT_ROWS)[...] = bot
                return carry
            lax.fori_loop(0, block_rows // COMBINE_UNROLL, combine, 0)

            rows = pl.ds(pl.multiple_of(r * block_rows, block_rows), block_rows)
            routed = jnp.concatenate([s[...] for buf in (top_buf, bot_buf) for s in slabs(buf, 0, block_rows)],
                                     axis=-1)
            x = x_ref[rows, :]
            xb = x.astype(BF16)
            hs = jax.nn.silu(_dot(xb, wsg_ref[...])) * _dot(xb, wsu_ref[...])
            shared = _dot(hs.astype(BF16), wsd_ref[...])
            y_ref[rows, :] = _layer_norm(alpha * x + (routed + shared), g_ref[...], b_ref[...])
            return carry
        lax.fori_loop(0, n_blocks, finish, 0)


def _expert_row_tile(chunk, n_experts):
    mean = chunk * TOP_K / n_experts
    rows = mean + 3.0 * math.sqrt(mean * (1.0 - TOP_K / n_experts))
    return int(-(-rows // BF16_ROWS) * BF16_ROWS)


def _moe(x, wts, alpha, chunk):
    n_tok, d_model = x.shape
    n_experts, _, expert_dim = wts['w_e_gate'].shape
    assert d_model == 2 * SLOT_ROWS * LANES and n_tok % chunk == 0 and n_experts % EXPERTS_PER_STEP == 0
    gates, pos, seg = _moe_plan(x, wts, chunk)
    row_tile = _expert_row_tile(chunk, n_experts)
    block_rows = _pick_tile(chunk, 256)
    consts = [wts['w_s_gate'], wts['w_s_up'], wts['w_s_down'], wts['ln2_g'], wts['ln2_b']]
    once = pl.Buffered(1)
    kern = functools.partial(_expert_kernel, row_tile=row_tile, experts_per_step=EXPERTS_PER_STEP,
                             block_rows=block_rows, alpha=alpha)

    def w_spec(rows, cols):
        return pl.BlockSpec((EXPERTS_PER_STEP, rows, cols), lambda c, s: (s, 0, 0))

    def table_spec(n):
        return pl.BlockSpec((n,), lambda c, s: (c,), memory_space=pltpu.SMEM, pipeline_mode=once)

    return pl.pallas_call(
        kern,
        grid=(n_tok // chunk, n_experts // EXPERTS_PER_STEP),
        in_specs=[table_spec(chunk)] * (2 * TOP_K) + [table_spec(2 * n_experts),
                  pl.BlockSpec((chunk, d_model), lambda c, s: (c, 0), pipeline_mode=once),
                  w_spec(d_model, expert_dim), w_spec(d_model, expert_dim), w_spec(expert_dim, d_model)]
                 + [pl.BlockSpec(a.shape, lambda c, s: (0, 0), pipeline_mode=once) for a in consts],
        out_specs=pl.BlockSpec((chunk, d_model), lambda c, s: (c, 0), pipeline_mode=once),
        out_shape=jax.ShapeDtypeStruct((n_tok, d_model), F32),
        scratch_shapes=[pltpu.VMEM(((chunk * TOP_K + row_tile) * SLOT_ROWS, LANES), jnp.uint32),
                        pltpu.VMEM((block_rows * SLOT_ROWS, LANES), jnp.uint32),
                        pltpu.VMEM((block_rows * SLOT_ROWS, LANES), F32),
                        pltpu.VMEM((block_rows * SLOT_ROWS, LANES), F32)],
        compiler_params=pltpu.CompilerParams(
            dimension_semantics=("arbitrary", "arbitrary"), vmem_limit_bytes=EXPERT_VMEM_LIMIT_BYTES),
        name="moe_experts",
    )(*pos, *gates, seg, x, wts['w_e_gate'], wts['w_e_up'], wts['w_e_down'], *consts)


def _layer_weights(l, w_in, conv_w, conv_b, conv_ln_g, conv_ln_b, w_a, pool_w, pool_scale, w_b, w_c, w_out,
                   ln1_g, ln1_b, w_router, router_bias, w_e_gate, w_e_up, w_e_down, w_s_gate, w_s_up,
                   w_s_down, ln2_g, ln2_b):
    conv_dim = conv_w.shape[-1]
    pool_dim = pool_scale.shape[-1]
    attn_dim = w_c.shape[1]
    kv_dim = N_KV_HEADS * HEAD_DIM
    o1 = 2 * conv_dim
    o2 = o1 + pool_dim
    o3 = o2 + attn_dim
    o5 = o3 + 2 * kv_dim
    wi = w_in[l]
    row = lambda v: v[l][None, :].astype(F32)
    wr_t = w_router[l].T
    wr_hi = wr_t.astype(BF16)
    return {
        'w_cv': wi[:, :o1].astype(BF16), 'w_pl': wi[:, o1:o2].astype(BF16), 'w_q': wi[:, o2:o3].astype(BF16),
        'w_kv': wi[:, o3:o5].astype(BF16), 'w_g': wi[:, o5:].astype(BF16),
        'conv_w': conv_w[l], 'conv_b': row(conv_b), 'conv_ln_g': row(conv_ln_g), 'conv_ln_b': row(conv_ln_b),
        'w_a': w_a[l].astype(BF16), 'pool_w': pool_w[l].astype(BF16), 'pool_scale': row(pool_scale),
        'w_b': w_b[l].astype(BF16), 'w_c': w_c[l].astype(BF16), 'w_out': w_out[l].astype(BF16),
        'ln1_g': row(ln1_g), 'ln1_b': row(ln1_b),
        'wr_hi': wr_hi, 'wr_lo': (wr_t - wr_hi.astype(F32)).astype(BF16),
        'router_bias': router_bias[l][:, None].astype(F32),
        'w_e_gate': w_e_gate[l].astype(BF16), 'w_e_up': w_e_up[l].astype(BF16),
        'w_e_down': w_e_down[l].astype(BF16),
        'w_s_gate': w_s_gate[l].astype(BF16), 'w_s_up': w_s_up[l].astype(BF16),
        'w_s_down': w_s_down[l].astype(BF16),
        'ln2_g': row(ln2_g), 'ln2_b': row(ln2_b),
    }


def _pick_tile(n, target):
    t = min(n, target)
    while n % t:
        t //= 2
    return t


def kernel(x_prompt, x_sample, cache_conv, cache_pool, cache_k, cache_v, w_in, conv_w, conv_b, conv_ln_g,
           conv_ln_b, w_a, pool_w, pool_scale, w_b, attn_sinks, w_c, w_out, ln1_g, ln1_b, w_router,
           router_bias, w_e_gate, w_e_up, w_e_down, w_s_gate, w_s_up, w_s_down, ln2_g, ln2_b):
    depth = w_in.shape[0]
    batch, seq, d_model = x_prompt.shape
    n_seq, n_new, _ = x_sample.shape
    win_cache = cache_k.shape[2]
    alpha = (2 * depth) ** 0.25
    tq = _pick_tile(seq, 512)
    nb = _pick_tile(n_seq, LANES // n_new // 2)
    tm_prompt = _pick_tile(batch * seq, MOE_CHUNK_TOKENS)
    tm_sample = _pick_tile(n_seq * n_new, MOE_CHUNK_TOKENS)

    yp = x_prompt
    ys = jnp.transpose(x_sample, (1, 0, 2))
    outs = [[] for _ in range(8)]
    for l in range(depth):
        wts = _layer_weights(l, w_in, conv_w, conv_b, conv_ln_g, conv_ln_b, w_a, pool_w, pool_scale, w_b,
                             w_c, w_out, ln1_g, ln1_b, w_router, router_bias, w_e_gate, w_e_up, w_e_down,
                             w_s_gate, w_s_up, w_s_down, ln2_g, ln2_b)
        sinks = attn_sinks[l].astype(F32)

        yp, c1, p1, k1, v1 = _prompt_mixer(yp, sinks, wts, alpha, tq)
        yp = _moe(yp.reshape(batch * seq, d_model), wts, alpha, tm_prompt).reshape(batch, seq, d_model)

        ck = cache_k[l].reshape(n_seq, win_cache, LANES)
        cv = cache_v[l].reshape(n_seq, win_cache, LANES)
        ys, u2, hp2, k2, v2 = _sample_mixer(
            ys, jnp.transpose(cache_conv[l], (1, 0, 2)), jnp.transpose(cache_pool[l], (1, 0, 2)),
            ck, cv, sinks, wts, alpha, nb)
        ys = _moe(ys.reshape(n_new * n_seq, d_model), wts, alpha, tm_sample).reshape(n_new, n_seq, d_model)

        to_bl = lambda a: jnp.transpose(a, (1, 0, 2))
        outs[0].append(c1)
        outs[1].append(p1)
        outs[2].append(k1.reshape(batch, WINDOW, N_KV_HEADS, HEAD_DIM))
        outs[3].append(v1.reshape(batch, WINDOW, N_KV_HEADS, HEAD_DIM))
        outs[4].append(jnp.concatenate([cache_conv[l], to_bl(u2)], axis=1)[:, n_new:])
        outs[5].append(jnp.concatenate([cache_pool[l], to_bl(hp2)], axis=1)[:, n_new:])
        outs[6].append(jnp.concatenate([ck, to_bl(k2)], axis=1)[:, n_new:]
                       .reshape(n_seq, win_cache, N_KV_HEADS, HEAD_DIM))
        outs[7].append(jnp.concatenate([cv, to_bl(v2)], axis=1)[:, n_new:]
                       .reshape(n_seq, win_cache, N_KV_HEADS, HEAD_DIM))
    return (yp, jnp.transpose(ys, (1, 0, 2))) + tuple(jnp.stack(o) for o in outs)
```

```python
import functools
import math

import jax
import jax.numpy as jnp
from jax import lax
from jax.experimental import pallas as pl
from jax.experimental.pallas import tpu as pltpu

PAST_LEN = 16384
WINDOW = 128
BLOCK = 128
N_HEADS = 8
N_KV_HEADS = 2
GROUP = N_HEADS // N_KV_HEADS
HEAD_DIM = 64
POOL_WINDOWS = (2, 4, 8, 16)
N_EXPERT_GROUPS = 8
TOPK_GROUPS = 4
TOP_K = 8
ROUTED_SCALE = 2.5
LN_EPS = 1e-5
NEG_BIG = -1e30

LANES = 128
SUBLANES = 8
VMEM_LIMIT_BYTES = 56 * 1024 * 1024
EXPERT_VMEM_LIMIT_BYTES = 60 * 1024 * 1024
BF16_ROWS = 16
SLOT_ROWS = 4
PLAN_SUB_TOKENS = 1024
MOE_CHUNK_TOKENS = 2048
EXPERTS_PER_STEP = 2
Y_STAGE_BUFFERS = 2
SCATTER_UNROLL = 8
COMBINE_UNROLL = 4

BF16 = jnp.bfloat16
F32 = jnp.float32


def _dot(a, b):
    return jnp.dot(a, b, preferred_element_type=F32)


def _dot_nt(a, b):
    return lax.dot_general(a, b, (((1,), (1,)), ((), ())), preferred_element_type=F32)


def _layer_norm(x, g, b):
    mu = jnp.mean(x, axis=-1, keepdims=True)
    xc = x - mu
    var = jnp.mean(xc * xc, axis=-1, keepdims=True)
    return xc * lax.rsqrt(var + LN_EPS) * g + b


def _slope(head):
    return 2.0 ** (-8.0 * (head + 1) / N_HEADS)


def _dup_lanes(x, half):
    rolled = pltpu.roll(x, HEAD_DIM, axis=1)
    lane = lax.broadcasted_iota(jnp.int32, x.shape, 1)
    lo = lane < HEAD_DIM
    if half == 0:
        return jnp.where(lo, x, rolled)
    return jnp.where(lo, rolled, x)


def _attention_group(q_pairs, kk, vv, bias_fn, valid, sinks, heads):
    rows = q_pairs[0].shape[0]
    lane = lax.broadcasted_iota(jnp.int32, (rows, LANES), 1)
    lo = lane < HEAD_DIM
    stacked = []
    for qp in q_pairs:
        stacked.append(jnp.where(lo, qp, 0.0).astype(BF16))
        stacked.append(jnp.where(lo, 0.0, qp).astype(BF16))
    lhs = jnp.concatenate(stacked, axis=0)
    s_all = _dot_nt(lhs, kk)
    probs = []
    for a, head in enumerate(heads):
        s = s_all[a * rows:(a + 1) * rows] * (HEAD_DIM ** -0.5) - bias_fn(head)
        s = jnp.where(valid, s, NEG_BIG)
        sink = sinks[head]
        m = jnp.maximum(jnp.max(s, axis=-1, keepdims=True), sink)
        e = jnp.exp(s - m)
        denom = jnp.sum(e, axis=-1, keepdims=True) + jnp.exp(sink - m)
        probs.append((e / denom).astype(BF16))
    o_all = _dot(jnp.concatenate(probs, axis=0), vv)
    outs = []
    for pair in range(GROUP // 2):
        o_lo = o_all[(2 * pair) * rows:(2 * pair + 1) * rows]
        o_hi = o_all[(2 * pair + 1) * rows:(2 * pair + 2) * rows]
        outs.append(jnp.where(lo, o_lo, o_hi))
    return outs


def _pool_mix(pooled_groups, poolw_ref, pscale_ref):
    mixed = [_dot(p.astype(BF16), poolw_ref[g]) for g, p in enumerate(pooled_groups)]
    return jnp.concatenate(mixed, axis=-1) * pscale_ref[...]


def _merge_and_norm(x, xb, br_a, br_b, br_c, wg_ref, wout_ref, g_ref, b_ref, alpha):
    d_model = x.shape[-1]
    gates = jax.nn.sigmoid(_dot(xb, wg_ref[...]))
    merged = (gates[:, :d_model] * br_a + gates[:, d_model:2 * d_model] * br_b
              + gates[:, 2 * d_model:] * br_c)
    mix = _dot(merged.astype(BF16), wout_ref[...])
    return _layer_norm(alpha * x + mix, g_ref[...], b_ref[...])


def _prompt_mixer_kernel(sinks_ref, x_ref, wcv_ref, wpl_ref, wq_ref, wkv_ref, wg_ref,
                         convw_ref, convb_ref, clng_ref, clnb_ref, wa_ref,
                         poolw_ref, pscale_ref, wb_ref, wc_ref, wout_ref, ln1g_ref, ln1b_ref,
                         y_ref, oconv_ref, opool_ref, ok_ref, ov_ref,
                         ucat, pcat, kcat, vcat, *, tq, conv_hist, pool_hist, alpha):
    t = pl.program_id(1)
    conv_width = convw_ref.shape[0]
    conv_dim = convw_ref.shape[1]
    pool_dim = wpl_ref.shape[1]
    gdim = pool_dim // len(POOL_WINDOWS)
    conv_state = conv_width - 1
    pool_state = max(POOL_WINDOWS) - 1

    @pl.when(t == 0)
    def _():
        ucat[0:conv_hist, :] = jnp.zeros((conv_hist, conv_dim), F32)
        pcat[0:pool_hist, :] = jnp.zeros((pool_hist, pool_dim), F32)
        kcat[0:BLOCK, :] = jnp.zeros((BLOCK, LANES), F32)
        vcat[0:BLOCK, :] = jnp.zeros((BLOCK, LANES), F32)

    x = x_ref[0]
    xb = x.astype(BF16)

    hc = _dot(xb, wcv_ref[...])
    u = hc[:, :conv_dim] * jax.nn.sigmoid(hc[:, conv_dim:])
    ucat[conv_hist:conv_hist + tq, :] = u
    acc = jnp.zeros((tq, conv_dim), F32) + convb_ref[...]
    base = conv_hist - conv_state
    window = ucat[...]
    n_rows = window.shape[0]
    rotated = {0: window}
    for j in range(conv_width):
        start, shift = (base + j) // SUBLANES * SUBLANES, (base + j) % SUBLANES
        if shift not in rotated:
            rotated[shift] = pltpu.roll(window, n_rows - shift, axis=0)
        acc = acc + convw_ref[j:j + 1, :] * rotated[shift][start:start + tq]
    c = _layer_norm(acc, clng_ref[...], clnb_ref[...])
    c = c * jax.nn.sigmoid(c)
    br_a = _dot(c.astype(BF16), wa_ref[...])
    oconv_ref[0] = ucat[conv_hist + tq - conv_state:conv_hist + tq, :]
    ucat[0:conv_hist, :] = ucat[tq:tq + conv_hist, :]

    hp = _dot(xb, wpl_ref[...])
    pcat[pool_hist:pool_hist + tq, :] = hp
    pos = t * tq + lax.broadcasted_iota(jnp.int32, (tq, gdim), 0)
    pooled = []
    for g, w in enumerate(POOL_WINDOWS):
        cols = slice(g * gdim, (g + 1) * gdim)
        s = pcat[pool_hist:pool_hist + tq, cols]
        cur = s
        for j in range(1, w):
            s = s + pcat[pool_hist - j:pool_hist - j + tq, cols]
        cnt = jnp.minimum(w, pos + 1).astype(F32)
        pooled.append(s / cnt - cur)
    br_b = _dot(_pool_mix(pooled, poolw_ref, pscale_ref).astype(BF16), wb_ref[...])
    opool_ref[0] = pcat[pool_hist + tq - pool_state:pool_hist + tq, :]
    pcat[0:pool_hist, :] = pcat[tq:tq + pool_hist, :]

    q = _dot(xb, wq_ref[...])
    kv = _dot(xb, wkv_ref[...])
    kcat[BLOCK:BLOCK + tq, :] = kv[:, :LANES]
    vcat[BLOCK:BLOCK + tq, :] = kv[:, LANES:]
    sinks = [sinks_ref[h] for h in range(N_HEADS)]
    qi = lax.broadcasted_iota(jnp.int32, (BLOCK, 2 * BLOCK), 0)
    kj = lax.broadcasted_iota(jnp.int32, (BLOCK, 2 * BLOCK), 1)
    dist_i = qi + BLOCK - kj
    dist = dist_i.astype(F32)
    in_window = (dist_i >= 0) & (dist_i <= WINDOW)
    first_key = jnp.where(t == 0, BLOCK, 0)
    att_blocks = []
    for b in range(tq // BLOCK):
        valid = in_window & (kj >= first_key) if b == 0 else in_window
        kblk = kcat[b * BLOCK:(b + 2) * BLOCK, :].astype(BF16)
        vblk = vcat[b * BLOCK:(b + 2) * BLOCK, :].astype(BF16)
        qblk = q[b * BLOCK:(b + 1) * BLOCK]
        cols = []
        for kvh in range(N_KV_HEADS):
            heads = [kvh * GROUP + a for a in range(GROUP)]
            q_pairs = [qblk[:, (kvh * GROUP + 2 * p) * HEAD_DIM:(kvh * GROUP + 2 * p + 2) * HEAD_DIM]
                       for p in range(GROUP // 2)]
            cols += _attention_group(q_pairs, _dup_lanes(kblk, kvh), _dup_lanes(vblk, kvh),
                                     lambda h: _slope(h) * dist, valid, sinks, heads)
        att_blocks.append(jnp.concatenate(cols, axis=-1))
    att = jnp.concatenate(att_blocks, axis=0)
    br_c = _dot(att.astype(BF16), wc_ref[...])
    ok_ref[0] = kcat[tq:tq + BLOCK, :]
    ov_ref[0] = vcat[tq:tq + BLOCK, :]
    kcat[0:BLOCK, :] = kcat[tq:tq + BLOCK, :]
    vcat[0:BLOCK, :] = vcat[tq:tq + BLOCK, :]

    y_ref[0] = _merge_and_norm(x, xb, br_a, br_b, br_c, wg_ref, wout_ref, ln1g_ref, ln1b_ref, alpha)


def _const_spec(arr):
    nd = arr.ndim
    return pl.BlockSpec(arr.shape, lambda *_: (0,) * nd)


def _prompt_mixer(x, sinks, wts, alpha, tq):
    batch, seq, d_model = x.shape
    conv_width, conv_dim = wts['conv_w'].shape
    pool_dim = wts['w_pl'].shape[1]
    conv_state = conv_width - 1
    pool_state = max(POOL_WINDOWS) - 1
    conv_hist = -(-conv_state // SUBLANES) * SUBLANES
    pool_hist = -(-pool_state // SUBLANES) * SUBLANES
    assert seq % tq == 0 and tq % BLOCK == 0 and tq >= max(conv_hist, pool_hist, BLOCK)
    names = ['w_cv', 'w_pl', 'w_q', 'w_kv', 'w_g', 'conv_w', 'conv_b', 'conv_ln_g', 'conv_ln_b', 'w_a',
             'pool_w', 'pool_scale', 'w_b', 'w_c', 'w_out', 'ln1_g', 'ln1_b']
    consts = [wts[n] for n in names]
    kern = functools.partial(_prompt_mixer_kernel, tq=tq, conv_hist=conv_hist, pool_hist=pool_hist,
                             alpha=alpha)
    out_shapes = (
        jax.ShapeDtypeStruct((batch, seq, d_model), F32),
        jax.ShapeDtypeStruct((batch, conv_state, conv_dim), F32),
        jax.ShapeDtypeStruct((batch, pool_state, pool_dim), F32),
        jax.ShapeDtypeStruct((batch, WINDOW, LANES), F32),
        jax.ShapeDtypeStruct((batch, WINDOW, LANES), F32),
    )
    return pl.pallas_call(
        kern,
        grid=(batch, seq // tq),
        in_specs=[pl.BlockSpec(memory_space=pltpu.SMEM),
                  pl.BlockSpec((1, tq, d_model), lambda b, t: (b, t, 0))]
                 + [_const_spec(a) for a in consts],
        out_specs=(
            pl.BlockSpec((1, tq, d_model), lambda b, t: (b, t, 0)),
            pl.BlockSpec((1, conv_state, conv_dim), lambda b, t: (b, 0, 0)),
            pl.BlockSpec((1, pool_state, pool_dim), lambda b, t: (b, 0, 0)),
            pl.BlockSpec((1, WINDOW, LANES), lambda b, t: (b, 0, 0)),
            pl.BlockSpec((1, WINDOW, LANES), lambda b, t: (b, 0, 0)),
        ),
        out_shape=out_shapes,
        scratch_shapes=[
            pltpu.VMEM((conv_hist + tq, conv_dim), F32),
            pltpu.VMEM((pool_hist + tq, pool_dim), F32),
            pltpu.VMEM((BLOCK + tq, LANES), F32),
            pltpu.VMEM((BLOCK + tq, LANES), F32),
        ],
        compiler_params=pltpu.CompilerParams(
            dimension_semantics=("arbitrary", "arbitrary"), vmem_limit_bytes=VMEM_LIMIT_BYTES),
        name="prompt_mixer",
    )(sinks, x, *consts)


def _sample_mixer_kernel(sinks_ref, x_ref, cconv_ref, cpool_ref, ck_ref, cv_ref,
                         wcv_ref, wpl_ref, wq_ref, wkv_ref, wg_ref,
                         convw_ref, convb_ref, clng_ref, clnb_ref, wa_ref,
                         poolw_ref, pscale_ref, wb_ref, wc_ref, wout_ref, ln1g_ref, ln1b_ref,
                         y_ref, ou_ref, op_ref, ok_ref, ov_ref, *, alpha):
    n_new, nb, d_model = x_ref.shape
    conv_width, conv_dim = convw_ref.shape
    conv_state = conv_width - 1
    pool_dim = wpl_ref.shape[1]
    gdim = pool_dim // len(POOL_WINDOWS)
    pool_state = max(POOL_WINDOWS) - 1
    win_cache = ck_ref.shape[1]
    rows = n_new * nb

    x = x_ref[...].reshape(rows, d_model)
    xb = x.astype(BF16)

    hc = _dot(xb, wcv_ref[...])
    u = hc[:, :conv_dim] * jax.nn.sigmoid(hc[:, conv_dim:])
    u_l = [u[l * nb:(l + 1) * nb] for l in range(n_new)]
    conv_rows = []
    for l in range(n_new):
        acc = jnp.zeros((nb, conv_dim), F32) + convb_ref[...]
        for j in range(conv_width):
            i = l + j
            src = cconv_ref[i] if i < conv_state else u_l[i - conv_state]
            acc = acc + convw_ref[j:j + 1, :] * src
        conv_rows.append(acc)
    c = _layer_norm(jnp.concatenate(conv_rows, axis=0), clng_ref[...], clnb_ref[...])
    c = c * jax.nn.sigmoid(c)
    br_a = _dot(c.astype(BF16), wa_ref[...])
    ou_ref[...] = u.reshape(n_new, nb, conv_dim)

    hp = _dot(xb, wpl_ref[...])
    hp_l = [hp[l * nb:(l + 1) * nb] for l in range(n_new)]
    pooled_rows = []
    for l in range(n_new):
        groups = []
        for g, w in enumerate(POOL_WINDOWS):
            cols = slice(g * gdim, (g + 1) * gdim)
            s = hp_l[l][:, cols]
            for j in range(1, w):
                i = l - j
                s = s + (hp_l[i][:, cols] if i >= 0 else cpool_ref[pool_state + i][:, cols])
            cnt = float(min(w, PAST_LEN + l + 1))
            groups.append(s / cnt - hp_l[l][:, cols])
        pooled_rows.append(groups)
    pooled = [jnp.concatenate([pooled_rows[l][g] for l in range(n_new)], axis=0)
              for g in range(len(POOL_WINDOWS))]
    br_b = _dot(_pool_mix(pooled, poolw_ref, pscale_ref).astype(BF16), wb_ref[...])
    op_ref[...] = hp.reshape(n_new, nb, pool_dim)

    q = _dot(xb, wq_ref[...])
    kv = _dot(xb, wkv_ref[...])
    k_new = kv[:, :LANES]
    v_new = kv[:, LANES:]
    ok_ref[...] = k_new.reshape(n_new, nb, LANES)
    ov_ref[...] = v_new.reshape(n_new, nb, LANES)
    n_cached = nb * win_cache
    pad = jnp.zeros((LANES - rows, LANES), BF16)
    k_all = jnp.concatenate([ck_ref[...].reshape(n_cached, LANES).astype(BF16), k_new.astype(BF16), pad], axis=0)
    v_all = jnp.concatenate([cv_ref[...].reshape(n_cached, LANES).astype(BF16), v_new.astype(BF16), pad], axis=0)
    n_keys = n_cached + LANES
    r = lax.broadcasted_iota(jnp.int32, (rows, n_keys), 0)
    col = lax.broadcasted_iota(jnp.int32, (rows, n_keys), 1)
    q_l = r // nb
    q_b = r % nb
    cached = col < n_cached
    cn = col - n_cached
    k_b = jnp.where(cached, col // win_cache, cn % nb)
    dist_i = jnp.where(cached, q_l + win_cache - col % win_cache, q_l - cn // nb)
    valid = (k_b == q_b) & (dist_i >= 0) & (dist_i <= WINDOW) & (cached | (cn < rows))
    dist = dist_i.astype(F32)
    sinks = [sinks_ref[h] for h in range(N_HEADS)]
    cols = []
    for kvh in range(N_KV_HEADS):
        heads = [kvh * GROUP + a for a in range(GROUP)]
        q_pairs = [q[:, (kvh * GROUP + 2 * p) * HEAD_DIM:(kvh * GROUP + 2 * p + 2) * HEAD_DIM]
                   for p in range(GROUP // 2)]
        cols += _attention_group(q_pairs, _dup_lanes(k_all, kvh), _dup_lanes(v_all, kvh),
                                 lambda h: _slope(h) * dist, valid, sinks, heads)
    att = jnp.concatenate(cols, axis=-1)
    br_c = _dot(att.astype(BF16), wc_ref[...])

    y = _merge_and_norm(x, xb, br_a, br_b, br_c, wg_ref, wout_ref, ln1g_ref, ln1b_ref, alpha)
    y_ref[...] = y.reshape(n_new, nb, d_model)


def _sample_mixer(x_lb, cconv_t, cpool_t, ck, cv, sinks, wts, alpha, nb):
    n_new, n_seq, d_model = x_lb.shape
    conv_state, _, conv_dim = cconv_t.shape
    pool_state, _, pool_dim = cpool_t.shape
    win_cache = ck.shape[1]
    assert n_seq % nb == 0 and nb % SUBLANES == 0 and n_new * nb <= LANES
    names = ['w_cv', 'w_pl', 'w_q', 'w_kv', 'w_g', 'conv_w', 'conv_b', 'conv_ln_g', 'conv_ln_b', 'w_a',
             'pool_w', 'pool_scale', 'w_b', 'w_c', 'w_out', 'ln1_g', 'ln1_b']
    consts = [wts[n] for n in names]

    def lb_spec(rows, width):
        return pl.BlockSpec((rows, nb, width), lambda c: (0, c, 0))

    out_shapes = (
        jax.ShapeDtypeStruct((n_new, n_seq, d_model), F32),
        jax.ShapeDtypeStruct((n_new, n_seq, conv_dim), F32),
        jax.ShapeDtypeStruct((n_new, n_seq, pool_dim), F32),
        jax.ShapeDtypeStruct((n_new, n_seq, LANES), F32),
        jax.ShapeDtypeStruct((n_new, n_seq, LANES), F32),
    )
    return pl.pallas_call(
        functools.partial(_sample_mixer_kernel, alpha=alpha),
        grid=(n_seq // nb,),
        in_specs=[pl.BlockSpec(memory_space=pltpu.SMEM),
                  lb_spec(n_new, d_model), lb_spec(conv_state, conv_dim), lb_spec(pool_state, pool_dim),
                  pl.BlockSpec((nb, win_cache, LANES), lambda c: (c, 0, 0)),
                  pl.BlockSpec((nb, win_cache, LANES), lambda c: (c, 0, 0))]
                 + [_const_spec(a) for a in consts],
        out_specs=(lb_spec(n_new, d_model), lb_spec(n_new, conv_dim), lb_spec(n_new, pool_dim),
                   lb_spec(n_new, LANES), lb_spec(n_new, LANES)),
        out_shape=out_shapes,
        compiler_params=pltpu.CompilerParams(
            dimension_semantics=("arbitrary",), vmem_limit_bytes=VMEM_LIMIT_BYTES),
        name="sample_mixer",
    )(sinks, x_lb, cconv_t, cpool_t, ck, cv, *consts)


def _route(x, wr_hi_ref, wr_lo_ref, rbias_ref):
    n_experts = wr_hi_ref.shape[0]
    per_group = n_experts // N_EXPERT_GROUPS
    tm = x.shape[0]
    x_hi = x.astype(BF16)
    x_lo = (x - x_hi.astype(F32)).astype(BF16)
    logits = (_dot_nt(wr_hi_ref[...], x_hi) + _dot_nt(wr_lo_ref[...], x_hi)
              + _dot_nt(wr_hi_ref[...], x_lo))
    scores = jax.nn.sigmoid(logits)
    sel = scores + rbias_ref[...]
    sub = lax.broadcasted_iota(jnp.int32, (per_group, tm), 0)
    grp_scores = []
    for g in range(N_EXPERT_GROUPS):
        v = sel[g * per_group:(g + 1) * per_group]
        m1 = jnp.max(v, axis=0, keepdims=True)
        first = jnp.min(jnp.where(v == m1, sub, per_group), axis=0, keepdims=True)
        m2 = jnp.max(jnp.where(sub == first, -jnp.inf, v), axis=0, keepdims=True)
        grp_scores.append(m1 + m2)
    keep = []
    for g in range(N_EXPERT_GROUPS):
        rank = jnp.zeros((1, tm), jnp.int32)
        for o in range(N_EXPERT_GROUPS):
            if o == g:
                continue
            ahead = (grp_scores[o] >= grp_scores[g]) if o < g else (grp_scores[o] > grp_scores[g])
            rank = rank + ahead.astype(jnp.int32)
        keep.append(jnp.broadcast_to(rank, (per_group, tm)))
    cand = jnp.where(jnp.concatenate(keep, axis=0) < TOPK_GROUPS, sel, -jnp.inf)
    eidx = lax.broadcasted_iota(jnp.int32, (n_experts, tm), 0)
    chosen = jnp.zeros((n_experts, tm), F32)
    picks, weights = [], []
    for _ in range(TOP_K):
        m = jnp.max(cand, axis=0, keepdims=True)
        first = jnp.min(jnp.where(cand == m, eidx, n_experts), axis=0, keepdims=True)
        pick = eidx == first
        picks.append(first)
        weights.append(jnp.sum(jnp.where(pick, scores, 0.0), axis=0, keepdims=True))
        chosen = jnp.where(pick, 1.0, chosen)
        cand = jnp.where(pick, -jnp.inf, cand)
    total = weights[0]
    for w in weights[1:]:
        total = total + w
    gates = [w / total * ROUTED_SCALE for w in weights]
    return picks, gates, chosen


def _plan_kernel(x_ref, wrh_ref, wrl_ref, rbias_ref, gates_ref, pos_ref, cnt_ref, off_ref):
    n_experts = wrh_ref.shape[0]
    tm = x_ref.shape[0]
    sub = min(tm, PLAN_SUB_TOKENS)
    picks, gates, chosen = _route(x_ref[...], wrh_ref, wrl_ref, rbias_ref)
    chosen_b = chosen.astype(BF16)
    before = (lax.broadcasted_iota(jnp.int32, (sub, sub), 0)
              < lax.broadcasted_iota(jnp.int32, (sub, sub), 1)).astype(BF16)
    ranks = []
    cnt = jnp.zeros((n_experts, 1), F32)
    for b in range(tm // sub):
        ranks.append(_dot(chosen_b[:, b * sub:(b + 1) * sub], before) + cnt)
        cnt = cnt + jnp.sum(chosen[:, b * sub:(b + 1) * sub], axis=1, keepdims=True)
    rank_in_expert = jnp.concatenate(ranks, axis=1)
    lower = (lax.broadcasted_iota(jnp.int32, (n_experts, n_experts), 1)
             < lax.broadcasted_iota(jnp.int32, (n_experts, n_experts), 0)).astype(BF16)
    off = jnp.sum(_dot(lower, chosen_b), axis=1, keepdims=True)
    rank = off + rank_in_expert
    eidx = lax.broadcasted_iota(jnp.int32, (n_experts, tm), 0)
    pos = [jnp.sum(jnp.where(eidx == p, rank, 0.0), axis=0, keepdims=True) for p in picks]
    gates_ref[0] = jnp.concatenate(gates, axis=0)
    pos_ref[0] = (jnp.concatenate(pos, axis=0) * SLOT_ROWS).astype(jnp.int32)
    cnt_ref[0] = jnp.broadcast_to(cnt, (n_experts, LANES)).astype(jnp.int32)
    off_ref[0] = jnp.broadcast_to(off, (n_experts, LANES)).astype(jnp.int32)


def _moe_plan(x, wts, chunk):
    n_tok, d_model = x.shape
    n_experts = wts['wr_hi'].shape[0]
    n_chunks = n_tok // chunk
    consts = [wts['wr_hi'], wts['wr_lo'], wts['router_bias']]
    gates, pos, cnt, off = pl.pallas_call(
        _plan_kernel,
        grid=(n_chunks,),
        in_specs=[pl.BlockSpec((chunk, d_model), lambda c: (c, 0))] + [_const_spec(a) for a in consts],
        out_specs=(pl.BlockSpec((1, TOP_K, chunk), lambda c: (c, 0, 0)),
                   pl.BlockSpec((1, TOP_K, chunk), lambda c: (c, 0, 0)),
                   pl.BlockSpec((1, n_experts, LANES), lambda c: (c, 0, 0)),
                   pl.BlockSpec((1, n_experts, LANES), lambda c: (c, 0, 0))),
        out_shape=(jax.ShapeDtypeStruct((n_chunks, TOP_K, chunk), F32),
                   jax.ShapeDtypeStruct((n_chunks, TOP_K, chunk), jnp.int32),
                   jax.ShapeDtypeStruct((n_chunks, n_experts, LANES), jnp.int32),
                   jax.ShapeDtypeStruct((n_chunks, n_experts, LANES), jnp.int32)),
        compiler_params=pltpu.CompilerParams(
            dimension_semantics=("arbitrary",), vmem_limit_bytes=VMEM_LIMIT_BYTES),
        name="moe_plan",
    )(x, *consts)
    seg = jnp.concatenate([cnt[:, :, 0], off[:, :, 0]], axis=1)
    per_pick = lambda a: [a[:, k, :].reshape(-1) for k in range(TOP_K)]
    return per_pick(gates), per_pick(pos), seg.reshape(-1)


def _expert_kernel(*refs, row_tile, experts_per_step, block_rows, alpha):
    pos_refs, gate_refs = refs[:TOP_K], refs[TOP_K:2 * TOP_K]
    (seg_ref, x_ref, weg_ref, weu_ref, wed_ref, wsg_ref, wsu_ref, wsd_ref, g_ref, b_ref,
     y_hbm, sbuf, pbuf, top_buf, bot_buf, y_stage, y_sem) = refs[2 * TOP_K:]
    step = pl.program_id(1)
    tokens, d_model = x_ref.shape
    half = d_model // 2
    n_experts = seg_ref.shape[0] // 2
    pairs = tokens * TOP_K
    n_blocks = tokens // block_rows

    def pack(a, b):
        return pltpu.pack_elementwise([a, b], packed_dtype=BF16)

    def unpack(w):
        return tuple(pltpu.unpack_elementwise(w, index=i, packed_dtype=BF16, unpacked_dtype=F32)
                     for i in range(2))

    def slot_at(ref, row):
        return ref.at[pl.ds(pl.multiple_of(row, SLOT_ROWS), SLOT_ROWS), :]

    def slabs(ref, first_row, rows):
        return [ref.at[pl.ds(first_row + j, rows, stride=SLOT_ROWS), :] for j in range(SLOT_ROWS)]

    @pl.when(step == 0)
    def _():
        sbuf[pairs * SLOT_ROWS:, :] = jnp.zeros((row_tile * SLOT_ROWS, LANES), jnp.uint32)

        def scatter_block(r, carry):
            rows = pl.ds(pl.multiple_of(r * block_rows, block_rows), block_rows)
            words = pack(x_ref[rows, :half], x_ref[rows, half:])
            for j, slab in enumerate(slabs(pbuf, 0, block_rows)):
                slab[...] = words[:, j * LANES:(j + 1) * LANES]

            def scatter(i, carry):
                for u in range(SCATTER_UNROLL):
                    local = i * SCATTER_UNROLL + u
                    t = r * block_rows + local
                    slot = slot_at(pbuf, local * SLOT_ROWS)[...]
                    for k in range(TOP_K):
                        slot_at(sbuf, pos_refs[k][t])[...] = slot
                return carry
            return lax.fori_loop(0, block_rows // SCATTER_UNROLL, scatter, carry)
        lax.fori_loop(0, n_blocks, scatter_block, 0)

    def unpack_rows(words):
        return jnp.concatenate(unpack(words), axis=-1)

    def hidden(words, j):
        lb = unpack_rows(words).astype(BF16)
        return (jax.nn.silu(_dot(lb, weg_ref[j])) * _dot(lb, weu_ref[j])).astype(BF16)

    def expert_rows(words, j):
        return _dot(hidden(words, j), wed_ref[j])

    def load_tile(base):
        return jnp.concatenate([s[...] for s in slabs(sbuf, base, row_tile)], axis=-1)

    def store_tile(base, words, y, n_live):
        live = lax.broadcasted_iota(jnp.int32, y.shape, 0) < n_live
        out = jnp.where(live, y, unpack_rows(words))
        new_words = pack(out[:, :half], out[:, half:])
        for j, slab in enumerate(slabs(sbuf, base, row_tile)):
            slab[...] = new_words[:, j * LANES:(j + 1) * LANES]

    experts = [step * experts_per_step + j for j in range(experts_per_step)]
    n_rows = [seg_ref[ex] for ex in experts]
    bases = [pl.multiple_of(seg_ref[n_experts + ex] * SLOT_ROWS, SLOT_ROWS) for ex in experts]
    lhs = [load_tile(b) for b in bases]
    hs = [hidden(l, j) for j, l in enumerate(lhs)]
    ys = [_dot(h, wed_ref[j]) for j, h in enumerate(hs)]
    for j in range(experts_per_step):
        store_tile(bases[j], lhs[j], ys[j], n_rows[j])

    for j in range(experts_per_step):
        def run_tile(i, carry, j=j):
            base = pl.multiple_of(bases[j] + i * (row_tile * SLOT_ROWS), SLOT_ROWS)
            tile = load_tile(base)
            store_tile(base, tile, expert_rows(tile, j), n_rows[j] - i * row_tile)
            return carry
        lax.fori_loop(1, (n_rows[j] + row_tile - 1) // row_tile, run_tile, 0)

    @pl.when(step == pl.num_programs(1) - 1)
    def _():
        chunk_row = pl.program_id(0) * tokens

        def y_copy(r, buf):
            rows = pl.ds(pl.multiple_of(chunk_row + r * block_rows, block_rows), block_rows)
            return pltpu.make_async_copy(y_stage.at[buf], y_hbm.at[rows, :], y_sem.at[buf])

        def finish(r, carry):
            buf = lax.rem(r, Y_STAGE_BUFFERS)

            @pl.when(r >= Y_STAGE_BUFFERS)
            def _():
                y_copy(r - Y_STAGE_BUFFERS, buf).wait()

            def combine(i, carry):
                for u in range(COMBINE_UNROLL):
                    local = i * COMBINE_UNROLL + u
                    t = r * block_rows + local
                    top = jnp.zeros((SLOT_ROWS, LANES), F32)
                    bot = jnp.zeros((SLOT_ROWS, LANES), F32)
                    for k in range(TOP_K):
                        a, b = unpack(slot_at(sbuf, pos_refs[k][t])[...])
                        gate = gate_refs[k][t]
                        top = top + gate * a
                        bot = bot + gate * b
                    slot_at(top_buf, local * SLOT_ROWS)[...] = top
                    slot_at(bot_buf, local * SLOT_ROWS)[...] = bot
                return carry
            lax.fori_loop(0, block_rows // COMBINE_UNROLL, combine, 0)

            rows = pl.ds(pl.multiple_of(r * block_rows, block_rows), block_rows)
            routed = jnp.concatenate([s[...] for buf in (top_buf, bot_buf) for s in slabs(buf, 0, block_rows)],
                                     axis=-1)
            x = x_ref[rows, :]
            xb = x.astype(BF16)
            hs = jax.nn.silu(_dot(xb, wsg_ref[...])) * _dot(xb, wsu_ref[...])
            shared = _dot(hs.astype(BF16), wsd_ref[...])
            y_stage[buf] = _layer_norm(alpha * x + (routed + shared), g_ref[...], b_ref[...])
            y_copy(r, buf).start()
            return carry
        lax.fori_loop(0, n_blocks, finish, 0)
        for r in range(max(n_blocks - Y_STAGE_BUFFERS, 0), n_blocks):
            y_copy(r, r % Y_STAGE_BUFFERS).wait()


def _expert_row_tile(chunk, n_experts):
    mean = chunk * TOP_K / n_experts
    rows = mean + 3.0 * math.sqrt(mean * (1.0 - TOP_K / n_experts))
    return int(-(-rows // BF16_ROWS) * BF16_ROWS)


def _moe(x, wts, alpha, chunk):
    n_tok, d_model = x.shape
    n_experts, _, expert_dim = wts['w_e_gate'].shape
    assert d_model == 2 * SLOT_ROWS * LANES and n_tok % chunk == 0 and n_experts % EXPERTS_PER_STEP == 0
    gates, pos, seg = _moe_plan(x, wts, chunk)
    row_tile = _expert_row_tile(chunk, n_experts)
    block_rows = _pick_tile(chunk, 256)
    consts = [wts['w_s_gate'], wts['w_s_up'], wts['w_s_down'], wts['ln2_g'], wts['ln2_b']]
    once = pl.Buffered(1)
    kern = functools.partial(_expert_kernel, row_tile=row_tile, experts_per_step=EXPERTS_PER_STEP,
                             block_rows=block_rows, alpha=alpha)

    def w_spec(rows, cols):
        return pl.BlockSpec((EXPERTS_PER_STEP, rows, cols), lambda c, s: (s, 0, 0))

    def table_spec(n):
        return pl.BlockSpec((n,), lambda c, s: (c,), memory_space=pltpu.SMEM, pipeline_mode=once)

    return pl.pallas_call(
        kern,
        grid=(n_tok // chunk, n_experts // EXPERTS_PER_STEP),
        in_specs=[table_spec(chunk)] * (2 * TOP_K) + [table_spec(2 * n_experts),
                  pl.BlockSpec((chunk, d_model), lambda c, s: (c, 0), pipeline_mode=once),
                  w_spec(d_model, expert_dim), w_spec(d_model, expert_dim), w_spec(expert_dim, d_model)]
                 + [pl.BlockSpec(a.shape, lambda c, s: (0, 0), pipeline_mode=once) for a in consts],
        out_specs=pl.BlockSpec(memory_space=pl.ANY),
        out_shape=jax.ShapeDtypeStruct((n_tok, d_model), F32),
        scratch_shapes=[pltpu.VMEM(((chunk * TOP_K + row_tile) * SLOT_ROWS, LANES), jnp.uint32),
                        pltpu.VMEM((block_rows * SLOT_ROWS, LANES), jnp.uint32),
                        pltpu.VMEM((block_rows * SLOT_ROWS, LANES), F32),
                        pltpu.VMEM((block_rows * SLOT_ROWS, LANES), F32),
                        pltpu.VMEM((Y_STAGE_BUFFERS, block_rows, d_model), F32),
                        pltpu.SemaphoreType.DMA((Y_STAGE_BUFFERS,))],
        compiler_params=pltpu.CompilerParams(
            dimension_semantics=("arbitrary", "arbitrary"), vmem_limit_bytes=EXPERT_VMEM_LIMIT_BYTES),
        name="moe_experts",
    )(*pos, *gates, seg, x, wts['w_e_gate'], wts['w_e_up'], wts['w_e_down'], *consts)


def _layer_weights(l, w_in, conv_w, conv_b, conv_ln_g, conv_ln_b, w_a, pool_w, pool_scale, w_b, w_c, w_out,
                   ln1_g, ln1_b, w_router, router_bias, w_e_gate, w_e_up, w_e_down, w_s_gate, w_s_up,
                   w_s_down, ln2_g, ln2_b):
    conv_dim = conv_w.shape[-1]
    pool_dim = pool_scale.shape[-1]
    attn_dim = w_c.shape[1]
    kv_dim = N_KV_HEADS * HEAD_DIM
    o1 = 2 * conv_dim
    o2 = o1 + pool_dim
    o3 = o2 + attn_dim
    o5 = o3 + 2 * kv_dim
    wi = w_in[l]
    row = lambda v: v[l][None, :].astype(F32)
    wr_t = w_router[l].T
    wr_hi = wr_t.astype(BF16)
    return {
        'w_cv': wi[:, :o1].astype(BF16), 'w_pl': wi[:, o1:o2].astype(BF16), 'w_q': wi[:, o2:o3].astype(BF16),
        'w_kv': wi[:, o3:o5].astype(BF16), 'w_g': wi[:, o5:].astype(BF16),
        'conv_w': conv_w[l], 'conv_b': row(conv_b), 'conv_ln_g': row(conv_ln_g), 'conv_ln_b': row(conv_ln_b),
        'w_a': w_a[l].astype(BF16), 'pool_w': pool_w[l].astype(BF16), 'pool_scale': row(pool_scale),
        'w_b': w_b[l].astype(BF16), 'w_c': w_c[l].astype(BF16), 'w_out': w_out[l].astype(BF16),
        'ln1_g': row(ln1_g), 'ln1_b': row(ln1_b),
        'wr_hi': wr_hi, 'wr_lo': (wr_t - wr_hi.astype(F32)).astype(BF16),
        'router_bias': router_bias[l][:, None].astype(F32),
        'w_e_gate': w_e_gate[l].astype(BF16), 'w_e_up': w_e_up[l].astype(BF16),
        'w_e_down': w_e_down[l].astype(BF16),
        'w_s_gate': w_s_gate[l].astype(BF16), 'w_s_up': w_s_up[l].astype(BF16),
        'w_s_down': w_s_down[l].astype(BF16),
        'ln2_g': row(ln2_g), 'ln2_b': row(ln2_b),
    }


def _pick_tile(n, target):
    t = min(n, target)
    while n % t:
        t //= 2
    return t


def kernel(x_prompt, x_sample, cache_conv, cache_pool, cache_k, cache_v, w_in, conv_w, conv_b, conv_ln_g,
           conv_ln_b, w_a, pool_w, pool_scale, w_b, attn_sinks, w_c, w_out, ln1_g, ln1_b, w_router,
           router_bias, w_e_gate, w_e_up, w_e_down, w_s_gate, w_s_up, w_s_down, ln2_g, ln2_b):
    depth = w_in.shape[0]
    batch, seq, d_model = x_prompt.shape
    n_seq, n_new, _ = x_sample.shape
    win_cache = cache_k.shape[2]
    alpha = (2 * depth) ** 0.25
    tq = _pick_tile(seq, 512)
    nb = _pick_tile(n_seq, LANES // n_new // 2)
    tm_prompt = _pick_tile(batch * seq, MOE_CHUNK_TOKENS)
    tm_sample = _pick_tile(n_seq * n_new, MOE_CHUNK_TOKENS)

    yp = x_prompt
    ys = jnp.transpose(x_sample, (1, 0, 2))
    outs = [[] for _ in range(8)]
    for l in range(depth):
        wts = _layer_weights(l, w_in, conv_w, conv_b, conv_ln_g, conv_ln_b, w_a, pool_w, pool_scale, w_b,
                             w_c, w_out, ln1_g, ln1_b, w_router, router_bias, w_e_gate, w_e_up, w_e_down,
                             w_s_gate, w_s_up, w_s_down, ln2_g, ln2_b)
        sinks = attn_sinks[l].astype(F32)

        yp, c1, p1, k1, v1 = _prompt_mixer(yp, sinks, wts, alpha, tq)
        yp = _moe(yp.reshape(batch * seq, d_model), wts, alpha, tm_prompt).reshape(batch, seq, d_model)

        ck = cache_k[l].reshape(n_seq, win_cache, LANES)
        cv = cache_v[l].reshape(n_seq, win_cache, LANES)
        ys, u2, hp2, k2, v2 = _sample_mixer(
            ys, jnp.transpose(cache_conv[l], (1, 0, 2)), jnp.transpose(cache_pool[l], (1, 0, 2)),
            ck, cv, sinks, wts, alpha, nb)
        ys = _moe(ys.reshape(n_new * n_seq, d_model), wts, alpha, tm_sample).reshape(n_new, n_seq, d_model)

        to_bl = lambda a: jnp.transpose(a, (1, 0, 2))
        outs[0].append(c1)
        outs[1].append(p1)
        outs[2].append(k1.reshape(batch, WINDOW, N_KV_HEADS, HEAD_DIM))
        outs[3].append(v1.reshape(batch, WINDOW, N_KV_HEADS, HEAD_DIM))
        outs[4].append(jnp.concatenate([cache_conv[l], to_bl(u2)], axis=1)[:, n_new:])
        outs[5].append(jnp.concatenate([cache_pool[l], to_bl(hp2)], axis=1)[:, n_new:])
        outs[6].append(jnp.concatenate([ck, to_bl(k2)], axis=1)[:, n_new:]
                       .reshape(n_seq, win_cache, N_KV_HEADS, HEAD_DIM))
        outs[7].append(jnp.concatenate([cv, to_bl(v2)], axis=1)[:, n_new:]
                       .reshape(n_seq, win_cache, N_KV_HEADS, HEAD_DIM))
    return (yp, jnp.transpose(ys, (1, 0, 2))) + tuple(jnp.stack(o) for o in outs)
```

```python
import functools
import math

import jax
import jax.numpy as jnp
from jax import lax
from jax.experimental import pallas as pl
from jax.experimental.pallas import tpu as pltpu

PAST_LEN = 16384
WINDOW = 128
BLOCK = 128
N_HEADS = 8
N_KV_HEADS = 2
GROUP = N_HEADS // N_KV_HEADS
HEAD_DIM = 64
POOL_WINDOWS = (2, 4, 8, 16)
N_EXPERT_GROUPS = 8
TOPK_GROUPS = 4
TOP_K = 8
ROUTED_SCALE = 2.5
LN_EPS = 1e-5
NEG_BIG = -1e30

LANES = 128
SUBLANES = 8
VMEM_LIMIT_BYTES = 56 * 1024 * 1024
EXPERT_VMEM_LIMIT_BYTES = 60 * 1024 * 1024
BF16_ROWS = 16
SLOT_ROWS = 4
PLAN_SUB_TOKENS = 1024
MOE_CHUNK_TOKENS = 2048
EXPERTS_PER_STEP = 2
Y_STAGE_BUFFERS = 2
PROMPT_SEQS_PER_STEP = 2
PROMPT_TILE_TOKENS = 256
SCATTER_UNROLL = 8
COMBINE_UNROLL = 4

BF16 = jnp.bfloat16
F32 = jnp.float32


def _dot(a, b):
    return jnp.dot(a, b, preferred_element_type=F32)


def _dot_nt(a, b):
    return lax.dot_general(a, b, (((1,), (1,)), ((), ())), preferred_element_type=F32)


def _layer_norm(x, g, b):
    mu = jnp.mean(x, axis=-1, keepdims=True)
    xc = x - mu
    var = jnp.mean(xc * xc, axis=-1, keepdims=True)
    return xc * lax.rsqrt(var + LN_EPS) * g + b


def _slope(head):
    return 2.0 ** (-8.0 * (head + 1) / N_HEADS)


def _dup_lanes(x, half):
    rolled = pltpu.roll(x, HEAD_DIM, axis=1)
    lane = lax.broadcasted_iota(jnp.int32, x.shape, 1)
    lo = lane < HEAD_DIM
    if half == 0:
        return jnp.where(lo, x, rolled)
    return jnp.where(lo, rolled, x)


def _attention_group(q_pairs, kk, vv, bias_fn, valid, sinks, heads):
    rows = q_pairs[0].shape[0]
    lane = lax.broadcasted_iota(jnp.int32, (rows, LANES), 1)
    lo = lane < HEAD_DIM
    stacked = []
    for qp in q_pairs:
        stacked.append(jnp.where(lo, qp, 0.0).astype(BF16))
        stacked.append(jnp.where(lo, 0.0, qp).astype(BF16))
    lhs = jnp.concatenate(stacked, axis=0)
    s_all = _dot_nt(lhs, kk)
    probs = []
    for a, head in enumerate(heads):
        s = s_all[a * rows:(a + 1) * rows] * (HEAD_DIM ** -0.5) - bias_fn(head)
        s = jnp.where(valid, s, NEG_BIG)
        sink = sinks[head]
        m = jnp.maximum(jnp.max(s, axis=-1, keepdims=True), sink)
        e = jnp.exp(s - m)
        denom = jnp.sum(e, axis=-1, keepdims=True) + jnp.exp(sink - m)
        probs.append((e / denom).astype(BF16))
    o_all = _dot(jnp.concatenate(probs, axis=0), vv)
    outs = []
    for pair in range(GROUP // 2):
        o_lo = o_all[(2 * pair) * rows:(2 * pair + 1) * rows]
        o_hi = o_all[(2 * pair + 1) * rows:(2 * pair + 2) * rows]
        outs.append(jnp.where(lo, o_lo, o_hi))
    return outs


def _pool_mix(pooled_groups, poolw_ref, pscale_ref):
    mixed = [_dot(p.astype(BF16), poolw_ref[g]) for g, p in enumerate(pooled_groups)]
    return jnp.concatenate(mixed, axis=-1) * pscale_ref[...]


def _merge_and_norm(x, gate_logits, br_a, br_b, br_c, wout_ref, g_ref, b_ref, alpha):
    d_model = x.shape[-1]
    gates = jax.nn.sigmoid(gate_logits)
    merged = (gates[:, :d_model] * br_a + gates[:, d_model:2 * d_model] * br_b
              + gates[:, 2 * d_model:] * br_c)
    mix = _dot(merged.astype(BF16), wout_ref[...])
    return _layer_norm(alpha * x + mix, g_ref[...], b_ref[...])


def _prompt_mixer_kernel(sinks_ref, x_ref, *refs, tq, conv_hist, pool_hist, alpha):
    weights, per_seq = refs[:17], refs[17:]
    stages = [_prompt_tile_stages(pl.program_id(1), sinks_ref, x_ref.at[i], *weights,
                                  *[r.at[i] for r in per_seq],
                                  tq=tq, conv_hist=conv_hist, pool_hist=pool_hist, alpha=alpha)
              for i in range(x_ref.shape[0])]
    while stages:
        stages = [s for s in stages if next(s, False)]


def _prompt_tile_stages(t, sinks_ref, x_ref, wcv_ref, wpl_ref, wq_ref, wkv_ref, wg_ref,
                        convw_ref, convb_ref, clng_ref, clnb_ref, wa_ref,
                        poolw_ref, pscale_ref, wb_ref, wc_ref, wout_ref, ln1g_ref, ln1b_ref,
                        y_ref, oconv_ref, opool_ref, ok_ref, ov_ref,
                        ucat, pcat, kcat, vcat, *, tq, conv_hist, pool_hist, alpha):
    conv_width = convw_ref.shape[0]
    conv_dim = convw_ref.shape[1]
    pool_dim = wpl_ref.shape[1]
    gdim = pool_dim // len(POOL_WINDOWS)
    conv_state = conv_width - 1
    pool_state = max(POOL_WINDOWS) - 1

    @pl.when(t == 0)
    def _():
        ucat[0:conv_hist, :] = jnp.zeros((conv_hist, conv_dim), F32)
        pcat[0:pool_hist, :] = jnp.zeros((pool_hist, pool_dim), F32)
        kcat[0:BLOCK, :] = jnp.zeros((BLOCK, LANES), F32)
        vcat[0:BLOCK, :] = jnp.zeros((BLOCK, LANES), F32)

    x = x_ref[...]
    xb = x.astype(BF16)

    hc = _dot(xb, wcv_ref[...])
    yield True
    u = hc[:, :conv_dim] * jax.nn.sigmoid(hc[:, conv_dim:])
    ucat[conv_hist:conv_hist + tq, :] = u
    gate_logits = _dot(xb, wg_ref[...])
    yield True
    acc = jnp.zeros((tq, conv_dim), F32) + convb_ref[...]
    base = conv_hist - conv_state
    window = ucat[...]
    n_rows = window.shape[0]
    rotated = {0: window}
    for j in range(conv_width):
        start, shift = (base + j) // SUBLANES * SUBLANES, (base + j) % SUBLANES
        if shift not in rotated:
            rotated[shift] = pltpu.roll(window, n_rows - shift, axis=0)
        acc = acc + convw_ref[j:j + 1, :] * rotated[shift][start:start + tq]
    hp = _dot(xb, wpl_ref[...])
    q = _dot(xb, wq_ref[...])
    kv = _dot(xb, wkv_ref[...])
    yield True
    c = _layer_norm(acc, clng_ref[...], clnb_ref[...])
    c = c * jax.nn.sigmoid(c)
    br_a = _dot(c.astype(BF16), wa_ref[...])
    oconv_ref[...] = ucat[conv_hist + tq - conv_state:conv_hist + tq, :]
    ucat[0:conv_hist, :] = ucat[tq:tq + conv_hist, :]
    yield True

    pcat[pool_hist:pool_hist + tq, :] = hp
    pos = t * tq + lax.broadcasted_iota(jnp.int32, (tq, gdim), 0)
    pooled = []
    for g, w in enumerate(POOL_WINDOWS):
        cols = slice(g * gdim, (g + 1) * gdim)
        s = pcat[pool_hist:pool_hist + tq, cols]
        cur = s
        for j in range(1, w):
            s = s + pcat[pool_hist - j:pool_hist - j + tq, cols]
        cnt = jnp.minimum(w, pos + 1).astype(F32)
        pooled.append(s / cnt - cur)
    br_b = _dot(_pool_mix(pooled, poolw_ref, pscale_ref).astype(BF16), wb_ref[...])
    opool_ref[...] = pcat[pool_hist + tq - pool_state:pool_hist + tq, :]
    pcat[0:pool_hist, :] = pcat[tq:tq + pool_hist, :]
    yield True

    kcat[BLOCK:BLOCK + tq, :] = kv[:, :LANES]
    vcat[BLOCK:BLOCK + tq, :] = kv[:, LANES:]
    sinks = [sinks_ref[h] for h in range(N_HEADS)]
    qi = lax.broadcasted_iota(jnp.int32, (BLOCK, 2 * BLOCK), 0)
    kj = lax.broadcasted_iota(jnp.int32, (BLOCK, 2 * BLOCK), 1)
    dist_i = qi + BLOCK - kj
    dist = dist_i.astype(F32)
    in_window = (dist_i >= 0) & (dist_i <= WINDOW)
    first_key = jnp.where(t == 0, BLOCK, 0)
    att_blocks = []
    for b in range(tq // BLOCK):
        valid = in_window & (kj >= first_key) if b == 0 else in_window
        kblk = kcat[b * BLOCK:(b + 2) * BLOCK, :].astype(BF16)
        vblk = vcat[b * BLOCK:(b + 2) * BLOCK, :].astype(BF16)
        qblk = q[b * BLOCK:(b + 1) * BLOCK]
        cols = []
        for kvh in range(N_KV_HEADS):
            heads = [kvh * GROUP + a for a in range(GROUP)]
            q_pairs = [qblk[:, (kvh * GROUP + 2 * p) * HEAD_DIM:(kvh * GROUP + 2 * p + 2) * HEAD_DIM]
                       for p in range(GROUP // 2)]
            cols += _attention_group(q_pairs, _dup_lanes(kblk, kvh), _dup_lanes(vblk, kvh),
                                     lambda h: _slope(h) * dist, valid, sinks, heads)
        att_blocks.append(jnp.concatenate(cols, axis=-1))
        yield True
    att = jnp.concatenate(att_blocks, axis=0)
    br_c = _dot(att.astype(BF16), wc_ref[...])
    ok_ref[...] = kcat[tq:tq + BLOCK, :]
    ov_ref[...] = vcat[tq:tq + BLOCK, :]
    kcat[0:BLOCK, :] = kcat[tq:tq + BLOCK, :]
    vcat[0:BLOCK, :] = vcat[tq:tq + BLOCK, :]
    yield True

    y_ref[...] = _merge_and_norm(x, gate_logits, br_a, br_b, br_c, wout_ref, ln1g_ref, ln1b_ref, alpha)


def _const_spec(arr):
    nd = arr.ndim
    return pl.BlockSpec(arr.shape, lambda *_: (0,) * nd, pipeline_mode=pl.Buffered(1))


def _prompt_mixer(x, sinks, wts, alpha, tq):
    batch, seq, d_model = x.shape
    conv_width, conv_dim = wts['conv_w'].shape
    pool_dim = wts['w_pl'].shape[1]
    conv_state = conv_width - 1
    pool_state = max(POOL_WINDOWS) - 1
    conv_hist = -(-conv_state // SUBLANES) * SUBLANES
    pool_hist = -(-pool_state // SUBLANES) * SUBLANES
    group = math.gcd(batch, PROMPT_SEQS_PER_STEP)
    assert seq % tq == 0 and tq % BLOCK == 0 and tq >= max(conv_hist, pool_hist, BLOCK)
    names = ['w_cv', 'w_pl', 'w_q', 'w_kv', 'w_g', 'conv_w', 'conv_b', 'conv_ln_g', 'conv_ln_b', 'w_a',
             'pool_w', 'pool_scale', 'w_b', 'w_c', 'w_out', 'ln1_g', 'ln1_b']
    consts = [wts[n] for n in names]
    kern = functools.partial(_prompt_mixer_kernel, tq=tq, conv_hist=conv_hist, pool_hist=pool_hist,
                             alpha=alpha)
    out_shapes = (
        jax.ShapeDtypeStruct((batch, seq, d_model), F32),
        jax.ShapeDtypeStruct((batch, conv_state, conv_dim), F32),
        jax.ShapeDtypeStruct((batch, pool_state, pool_dim), F32),
        jax.ShapeDtypeStruct((batch, WINDOW, LANES), F32),
        jax.ShapeDtypeStruct((batch, WINDOW, LANES), F32),
    )
    return pl.pallas_call(
        kern,
        grid=(batch // group, seq // tq),
        in_specs=[pl.BlockSpec(memory_space=pltpu.SMEM),
                  pl.BlockSpec((group, tq, d_model), lambda b, t: (b, t, 0))]
                 + [_const_spec(a) for a in consts],
        out_specs=(
            pl.BlockSpec((group, tq, d_model), lambda b, t: (b, t, 0)),
            pl.BlockSpec((group, conv_state, conv_dim), lambda b, t: (b, 0, 0)),
            pl.BlockSpec((group, pool_state, pool_dim), lambda b, t: (b, 0, 0)),
            pl.BlockSpec((group, WINDOW, LANES), lambda b, t: (b, 0, 0)),
            pl.BlockSpec((group, WINDOW, LANES), lambda b, t: (b, 0, 0)),
        ),
        out_shape=out_shapes,
        scratch_shapes=[
            pltpu.VMEM((group, conv_hist + tq, conv_dim), F32),
            pltpu.VMEM((group, pool_hist + tq, pool_dim), F32),
            pltpu.VMEM((group, BLOCK + tq, LANES), F32),
            pltpu.VMEM((group, BLOCK + tq, LANES), F32),
        ],
        compiler_params=pltpu.CompilerParams(
            dimension_semantics=("arbitrary", "arbitrary"), vmem_limit_bytes=VMEM_LIMIT_BYTES),
        name="prompt_mixer",
    )(sinks, x, *consts)


def _sample_mixer_kernel(sinks_ref, x_ref, cconv_ref, cpool_ref, ck_ref, cv_ref,
                         wcv_ref, wpl_ref, wq_ref, wkv_ref, wg_ref,
                         convw_ref, convb_ref, clng_ref, clnb_ref, wa_ref,
                         poolw_ref, pscale_ref, wb_ref, wc_ref, wout_ref, ln1g_ref, ln1b_ref,
                         y_ref, ou_ref, op_ref, ok_ref, ov_ref, *, alpha):
    n_new, nb, d_model = x_ref.shape
    conv_width, conv_dim = convw_ref.shape
    conv_state = conv_width - 1
    pool_dim = wpl_ref.shape[1]
    gdim = pool_dim // len(POOL_WINDOWS)
    pool_state = max(POOL_WINDOWS) - 1
    win_cache = ck_ref.shape[1]
    rows = n_new * nb

    x = x_ref[...].reshape(rows, d_model)
    xb = x.astype(BF16)

    hc = _dot(xb, wcv_ref[...])
    u = hc[:, :conv_dim] * jax.nn.sigmoid(hc[:, conv_dim:])
    u_l = [u[l * nb:(l + 1) * nb] for l in range(n_new)]
    conv_rows = []
    for l in range(n_new):
        acc = jnp.zeros((nb, conv_dim), F32) + convb_ref[...]
        for j in range(conv_width):
            i = l + j
            src = cconv_ref[i] if i < conv_state else u_l[i - conv_state]
            acc = acc + convw_ref[j:j + 1, :] * src
        conv_rows.append(acc)
    c = _layer_norm(jnp.concatenate(conv_rows, axis=0), clng_ref[...], clnb_ref[...])
    c = c * jax.nn.sigmoid(c)
    br_a = _dot(c.astype(BF16), wa_ref[...])
    ou_ref[...] = u.reshape(n_new, nb, conv_dim)

    hp = _dot(xb, wpl_ref[...])
    hp_l = [hp[l * nb:(l + 1) * nb] for l in range(n_new)]
    pooled_rows = []
    for l in range(n_new):
        groups = []
        for g, w in enumerate(POOL_WINDOWS):
            cols = slice(g * gdim, (g + 1) * gdim)
            s = hp_l[l][:, cols]
            for j in range(1, w):
                i = l - j
                s = s + (hp_l[i][:, cols] if i >= 0 else cpool_ref[pool_state + i][:, cols])
            cnt = float(min(w, PAST_LEN + l + 1))
            groups.append(s / cnt - hp_l[l][:, cols])
        pooled_rows.append(groups)
    pooled = [jnp.concatenate([pooled_rows[l][g] for l in range(n_new)], axis=0)
              for g in range(len(POOL_WINDOWS))]
    br_b = _dot(_pool_mix(pooled, poolw_ref, pscale_ref).astype(BF16), wb_ref[...])
    op_ref[...] = hp.reshape(n_new, nb, pool_dim)

    q = _dot(xb, wq_ref[...])
    kv = _dot(xb, wkv_ref[...])
    k_new = kv[:, :LANES]
    v_new = kv[:, LANES:]
    ok_ref[...] = k_new.reshape(n_new, nb, LANES)
    ov_ref[...] = v_new.reshape(n_new, nb, LANES)
    n_cached = nb * win_cache
    pad = jnp.zeros((LANES - rows, LANES), BF16)
    k_all = jnp.concatenate([ck_ref[...].reshape(n_cached, LANES).astype(BF16), k_new.astype(BF16), pad], axis=0)
    v_all = jnp.concatenate([cv_ref[...].reshape(n_cached, LANES).astype(BF16), v_new.astype(BF16), pad], axis=0)
    n_keys = n_cached + LANES
    r = lax.broadcasted_iota(jnp.int32, (rows, n_keys), 0)
    col = lax.broadcasted_iota(jnp.int32, (rows, n_keys), 1)
    q_l = r // nb
    q_b = r % nb
    cached = col < n_cached
    cn = col - n_cached
    k_b = jnp.where(cached, col // win_cache, cn % nb)
    dist_i = jnp.where(cached, q_l + win_cache - col % win_cache, q_l - cn // nb)
    valid = (k_b == q_b) & (dist_i >= 0) & (dist_i <= WINDOW) & (cached | (cn < rows))
    dist = dist_i.astype(F32)
    sinks = [sinks_ref[h] for h in range(N_HEADS)]
    cols = []
    for kvh in range(N_KV_HEADS):
        heads = [kvh * GROUP + a for a in range(GROUP)]
        q_pairs = [q[:, (kvh * GROUP + 2 * p) * HEAD_DIM:(kvh * GROUP + 2 * p + 2) * HEAD_DIM]
                   for p in range(GROUP // 2)]
        cols += _attention_group(q_pairs, _dup_lanes(k_all, kvh), _dup_lanes(v_all, kvh),
                                 lambda h: _slope(h) * dist, valid, sinks, heads)
    att = jnp.concatenate(cols, axis=-1)
    br_c = _dot(att.astype(BF16), wc_ref[...])

    y = _merge_and_norm(x, _dot(xb, wg_ref[...]), br_a, br_b, br_c, wout_ref, ln1g_ref, ln1b_ref, alpha)
    y_ref[...] = y.reshape(n_new, nb, d_model)


def _sample_mixer(x_lb, cconv_t, cpool_t, ck, cv, sinks, wts, alpha, nb):
    n_new, n_seq, d_model = x_lb.shape
    conv_state, _, conv_dim = cconv_t.shape
    pool_state, _, pool_dim = cpool_t.shape
    win_cache = ck.shape[1]
    assert n_seq % nb == 0 and nb % SUBLANES == 0 and n_new * nb <= LANES
    names = ['w_cv', 'w_pl', 'w_q', 'w_kv', 'w_g', 'conv_w', 'conv_b', 'conv_ln_g', 'conv_ln_b', 'w_a',
             'pool_w', 'pool_scale', 'w_b', 'w_c', 'w_out', 'ln1_g', 'ln1_b']
    consts = [wts[n] for n in names]

    def lb_spec(rows, width):
        return pl.BlockSpec((rows, nb, width), lambda c: (0, c, 0))

    out_shapes = (
        jax.ShapeDtypeStruct((n_new, n_seq, d_model), F32),
        jax.ShapeDtypeStruct((n_new, n_seq, conv_dim), F32),
        jax.ShapeDtypeStruct((n_new, n_seq, pool_dim), F32),
        jax.ShapeDtypeStruct((n_new, n_seq, LANES), F32),
        jax.ShapeDtypeStruct((n_new, n_seq, LANES), F32),
    )
    return pl.pallas_call(
        functools.partial(_sample_mixer_kernel, alpha=alpha),
        grid=(n_seq // nb,),
        in_specs=[pl.BlockSpec(memory_space=pltpu.SMEM),
                  lb_spec(n_new, d_model), lb_spec(conv_state, conv_dim), lb_spec(pool_state, pool_dim),
                  pl.BlockSpec((nb, win_cache, LANES), lambda c: (c, 0, 0)),
                  pl.BlockSpec((nb, win_cache, LANES), lambda c: (c, 0, 0))]
                 + [_const_spec(a) for a in consts],
        out_specs=(lb_spec(n_new, d_model), lb_spec(n_new, conv_dim), lb_spec(n_new, pool_dim),
                   lb_spec(n_new, LANES), lb_spec(n_new, LANES)),
        out_shape=out_shapes,
        compiler_params=pltpu.CompilerParams(
            dimension_semantics=("arbitrary",), vmem_limit_bytes=VMEM_LIMIT_BYTES),
        name="sample_mixer",
    )(sinks, x_lb, cconv_t, cpool_t, ck, cv, *consts)


def _route(x, wr_hi_ref, wr_lo_ref, rbias_ref):
    n_experts = wr_hi_ref.shape[0]
    per_group = n_experts // N_EXPERT_GROUPS
    tm = x.shape[0]
    x_hi = x.astype(BF16)
    x_lo = (x - x_hi.astype(F32)).astype(BF16)
    logits = (_dot_nt(wr_hi_ref[...], x_hi) + _dot_nt(wr_lo_ref[...], x_hi)
              + _dot_nt(wr_hi_ref[...], x_lo))
    scores = jax.nn.sigmoid(logits)
    sel = scores + rbias_ref[...]
    sub = lax.broadcasted_iota(jnp.int32, (per_group, tm), 0)
    grp_scores = []
    for g in range(N_EXPERT_GROUPS):
        v = sel[g * per_group:(g + 1) * per_group]
        m1 = jnp.max(v, axis=0, keepdims=True)
        first = jnp.min(jnp.where(v == m1, sub, per_group), axis=0, keepdims=True)
        m2 = jnp.max(jnp.where(sub == first, -jnp.inf, v), axis=0, keepdims=True)
        grp_scores.append(m1 + m2)
    keep = []
    for g in range(N_EXPERT_GROUPS):
        rank = jnp.zeros((1, tm), jnp.int32)
        for o in range(N_EXPERT_GROUPS):
            if o == g:
                continue
            ahead = (grp_scores[o] >= grp_scores[g]) if o < g else (grp_scores[o] > grp_scores[g])
            rank = rank + ahead.astype(jnp.int32)
        keep.append(jnp.broadcast_to(rank, (per_group, tm)))
    cand = jnp.where(jnp.concatenate(keep, axis=0) < TOPK_GROUPS, sel, -jnp.inf)
    eidx = lax.broadcasted_iota(jnp.int32, (n_experts, tm), 0)
    chosen = jnp.zeros((n_experts, tm), F32)
    picks, weights = [], []
    for _ in range(TOP_K):
        m = jnp.max(cand, axis=0, keepdims=True)
        first = jnp.min(jnp.where(cand == m, eidx, n_experts), axis=0, keepdims=True)
        pick = eidx == first
        picks.append(first)
        weights.append(jnp.sum(jnp.where(pick, scores, 0.0), axis=0, keepdims=True))
        chosen = jnp.where(pick, 1.0, chosen)
        cand = jnp.where(pick, -jnp.inf, cand)
    total = weights[0]
    for w in weights[1:]:
        total = total + w
    gates = [w / total * ROUTED_SCALE for w in weights]
    return picks, gates, chosen


def _plan_kernel(x_ref, wrh_ref, wrl_ref, rbias_ref, gates_ref, pos_ref, cnt_ref, off_ref):
    n_experts = wrh_ref.shape[0]
    tm = x_ref.shape[0]
    sub = min(tm, PLAN_SUB_TOKENS)
    picks, gates, chosen = _route(x_ref[...], wrh_ref, wrl_ref, rbias_ref)
    chosen_b = chosen.astype(BF16)
    before = (lax.broadcasted_iota(jnp.int32, (sub, sub), 0)
              < lax.broadcasted_iota(jnp.int32, (sub, sub), 1)).astype(BF16)
    ranks = []
    cnt = jnp.zeros((n_experts, 1), F32)
    for b in range(tm // sub):
        ranks.append(_dot(chosen_b[:, b * sub:(b + 1) * sub], before) + cnt)
        cnt = cnt + jnp.sum(chosen[:, b * sub:(b + 1) * sub], axis=1, keepdims=True)
    rank_in_expert = jnp.concatenate(ranks, axis=1)
    lower = (lax.broadcasted_iota(jnp.int32, (n_experts, n_experts), 1)
             < lax.broadcasted_iota(jnp.int32, (n_experts, n_experts), 0)).astype(BF16)
    off = jnp.sum(_dot(lower, chosen_b), axis=1, keepdims=True)
    rank = off + rank_in_expert
    eidx = lax.broadcasted_iota(jnp.int32, (n_experts, tm), 0)
    pos = [jnp.sum(jnp.where(eidx == p, rank, 0.0), axis=0, keepdims=True) for p in picks]
    for k in range(TOP_K):
        gates_ref[k] = gates[k]
        pos_ref[k] = (pos[k] * SLOT_ROWS).astype(jnp.int32)
    cnt_ref[0] = jnp.broadcast_to(cnt, (n_experts, LANES)).astype(jnp.int32)
    off_ref[0] = jnp.broadcast_to(off, (n_experts, LANES)).astype(jnp.int32)


def _moe_plan(x, wts, chunk):
    n_tok, d_model = x.shape
    n_experts = wts['wr_hi'].shape[0]
    n_chunks = n_tok // chunk
    consts = [wts['wr_hi'], wts['wr_lo'], wts['router_bias']]
    gates, pos, cnt, off = pl.pallas_call(
        _plan_kernel,
        grid=(n_chunks,),
        in_specs=[pl.BlockSpec((chunk, d_model), lambda c: (c, 0))] + [_const_spec(a) for a in consts],
        out_specs=(pl.BlockSpec((TOP_K, 1, chunk), lambda c: (0, 0, c)),
                   pl.BlockSpec((TOP_K, 1, chunk), lambda c: (0, 0, c)),
                   pl.BlockSpec((1, n_experts, LANES), lambda c: (c, 0, 0)),
                   pl.BlockSpec((1, n_experts, LANES), lambda c: (c, 0, 0))),
        out_shape=(jax.ShapeDtypeStruct((TOP_K, 1, n_tok), F32),
                   jax.ShapeDtypeStruct((TOP_K, 1, n_tok), jnp.int32),
                   jax.ShapeDtypeStruct((n_chunks, n_experts, LANES), jnp.int32),
                   jax.ShapeDtypeStruct((n_chunks, n_experts, LANES), jnp.int32)),
        compiler_params=pltpu.CompilerParams(
            dimension_semantics=("arbitrary",), vmem_limit_bytes=VMEM_LIMIT_BYTES),
        name="moe_plan",
    )(x, *consts)
    seg = jnp.concatenate([cnt[:, :, 0], off[:, :, 0]], axis=1)
    return gates, pos, seg.reshape(-1)


def _expert_kernel(*refs, row_tile, experts_per_step, block_rows, alpha):
    pos_refs, gate_refs = refs[:TOP_K], refs[TOP_K:2 * TOP_K]
    (seg_ref, x_ref, weg_ref, weu_ref, wed_ref, wsg_ref, wsu_ref, wsd_ref, g_ref, b_ref,
     y_hbm, sbuf, pbuf, top_buf, bot_buf, y_stage, y_sem) = refs[2 * TOP_K:]
    step = pl.program_id(1)
    tokens, d_model = x_ref.shape
    half = d_model // 2
    n_experts = seg_ref.shape[0] // 2
    pairs = tokens * TOP_K
    n_blocks = tokens // block_rows

    def pack(a, b):
        return pltpu.pack_elementwise([a, b], packed_dtype=BF16)

    def unpack(w):
        return tuple(pltpu.unpack_elementwise(w, index=i, packed_dtype=BF16, unpacked_dtype=F32)
                     for i in range(2))

    def slot_at(ref, row):
        return ref.at[pl.ds(pl.multiple_of(row, SLOT_ROWS), SLOT_ROWS), :]

    def slabs(ref, first_row, rows):
        return [ref.at[pl.ds(first_row + j, rows, stride=SLOT_ROWS), :] for j in range(SLOT_ROWS)]

    @pl.when(step == 0)
    def _():
        sbuf[pairs * SLOT_ROWS:, :] = jnp.zeros((row_tile * SLOT_ROWS, LANES), jnp.uint32)

        def scatter_block(r, carry):
            rows = pl.ds(pl.multiple_of(r * block_rows, block_rows), block_rows)
            words = pack(x_ref[rows, :half], x_ref[rows, half:])
            for j, slab in enumerate(slabs(pbuf, 0, block_rows)):
                slab[...] = words[:, j * LANES:(j + 1) * LANES]

            def scatter(i, carry):
                for u in range(SCATTER_UNROLL):
                    local = i * SCATTER_UNROLL + u
                    t = r * block_rows + local
                    slot = slot_at(pbuf, local * SLOT_ROWS)[...]
                    for k in range(TOP_K):
                        slot_at(sbuf, pos_refs[k][0, 0, t])[...] = slot
                return carry
            return lax.fori_loop(0, block_rows // SCATTER_UNROLL, scatter, carry)
        lax.fori_loop(0, n_blocks, scatter_block, 0)

    def unpack_rows(words):
        return jnp.concatenate(unpack(words), axis=-1)

    def hidden(words, j):
        lb = unpack_rows(words).astype(BF16)
        return (jax.nn.silu(_dot(lb, weg_ref[j])) * _dot(lb, weu_ref[j])).astype(BF16)

    def expert_rows(words, j):
        return _dot(hidden(words, j), wed_ref[j])

    def load_tile(base):
        return jnp.concatenate([s[...] for s in slabs(sbuf, base, row_tile)], axis=-1)

    def store_tile(base, words, y, n_live):
        live = lax.broadcasted_iota(jnp.int32, y.shape, 0) < n_live
        out = jnp.where(live, y, unpack_rows(words))
        new_words = pack(out[:, :half], out[:, half:])
        for j, slab in enumerate(slabs(sbuf, base, row_tile)):
            slab[...] = new_words[:, j * LANES:(j + 1) * LANES]

    experts = [step * experts_per_step + j for j in range(experts_per_step)]
    n_rows = [seg_ref[ex] for ex in experts]
    bases = [pl.multiple_of(seg_ref[n_experts + ex] * SLOT_ROWS, SLOT_ROWS) for ex in experts]
    lhs = [load_tile(b) for b in bases]
    hs = [hidden(l, j) for j, l in enumerate(lhs)]
    ys = [_dot(h, wed_ref[j]) for j, h in enumerate(hs)]
    for j in range(experts_per_step):
        store_tile(bases[j], lhs[j], ys[j], n_rows[j])

    for j in range(experts_per_step):
        def run_tile(i, carry, j=j):
            base = pl.multiple_of(bases[j] + i * (row_tile * SLOT_ROWS), SLOT_ROWS)
            tile = load_tile(base)
            store_tile(base, tile, expert_rows(tile, j), n_rows[j] - i * row_tile)
            return carry
        lax.fori_loop(1, (n_rows[j] + row_tile - 1) // row_tile, run_tile, 0)

    @pl.when(step == pl.num_programs(1) - 1)
    def _():
        chunk_row = pl.program_id(0) * tokens

        def y_copy(r, buf):
            rows = pl.ds(pl.multiple_of(chunk_row + r * block_rows, block_rows), block_rows)
            return pltpu.make_async_copy(y_stage.at[buf], y_hbm.at[rows, :], y_sem.at[buf])

        def finish(r, carry):
            buf = lax.rem(r, Y_STAGE_BUFFERS)

            @pl.when(r >= Y_STAGE_BUFFERS)
            def _():
                y_copy(r - Y_STAGE_BUFFERS, buf).wait()

            def combine(i, carry):
                for u in range(COMBINE_UNROLL):
                    local = i * COMBINE_UNROLL + u
                    t = r * block_rows + local
                    top = jnp.zeros((SLOT_ROWS, LANES), F32)
                    bot = jnp.zeros((SLOT_ROWS, LANES), F32)
                    for k in range(TOP_K):
                        a, b = unpack(slot_at(sbuf, pos_refs[k][0, 0, t])[...])
                        gate = gate_refs[k][0, 0, t]
                        top = top + gate * a
                        bot = bot + gate * b
                    slot_at(top_buf, local * SLOT_ROWS)[...] = top
                    slot_at(bot_buf, local * SLOT_ROWS)[...] = bot
                return carry
            lax.fori_loop(0, block_rows // COMBINE_UNROLL, combine, 0)

            rows = pl.ds(pl.multiple_of(r * block_rows, block_rows), block_rows)
            routed = jnp.concatenate([s[...] for buf in (top_buf, bot_buf) for s in slabs(buf, 0, block_rows)],
                                     axis=-1)
            x = x_ref[rows, :]
            xb = x.astype(BF16)
            hs = jax.nn.silu(_dot(xb, wsg_ref[...])) * _dot(xb, wsu_ref[...])
            shared = _dot(hs.astype(BF16), wsd_ref[...])
            y_stage[buf] = _layer_norm(alpha * x + (routed + shared), g_ref[...], b_ref[...])
            y_copy(r, buf).start()
            return carry
        lax.fori_loop(0, n_blocks, finish, 0)
        for r in range(max(n_blocks - Y_STAGE_BUFFERS, 0), n_blocks):
            y_copy(r, r % Y_STAGE_BUFFERS).wait()


def _expert_row_tile(chunk, n_experts):
    mean = chunk * TOP_K / n_experts
    rows = mean + 3.0 * math.sqrt(mean * (1.0 - TOP_K / n_experts))
    return int(-(-rows // BF16_ROWS) * BF16_ROWS)


def _moe(x, wts, alpha, chunk):
    n_tok, d_model = x.shape
    n_experts, _, expert_dim = wts['w_e_gate'].shape
    assert d_model == 2 * SLOT_ROWS * LANES and n_tok % chunk == 0 and n_experts % EXPERTS_PER_STEP == 0
    gates, pos, seg = _moe_plan(x, wts, chunk)
    row_tile = _expert_row_tile(chunk, n_experts)
    block_rows = _pick_tile(chunk, 256)
    consts = [wts['w_s_gate'], wts['w_s_up'], wts['w_s_down'], wts['ln2_g'], wts['ln2_b']]
    once = pl.Buffered(1)
    kern = functools.partial(_expert_kernel, row_tile=row_tile, experts_per_step=EXPERTS_PER_STEP,
                             block_rows=block_rows, alpha=alpha)

    def w_spec(rows, cols):
        return pl.BlockSpec((EXPERTS_PER_STEP, rows, cols), lambda c, s: (s, 0, 0))

    def pick_spec(k):
        return pl.BlockSpec((1, 1, chunk), lambda c, s: (k, 0, c), memory_space=pltpu.SMEM, pipeline_mode=once)

    picks = [pick_spec(k) for k in range(TOP_K)]
    return pl.pallas_call(
        kern,
        grid=(n_tok // chunk, n_experts // EXPERTS_PER_STEP),
        in_specs=picks + picks + [
                  pl.BlockSpec((2 * n_experts,), lambda c, s: (c,), memory_space=pltpu.SMEM, pipeline_mode=once),
                  pl.BlockSpec((chunk, d_model), lambda c, s: (c, 0), pipeline_mode=once),
                  w_spec(d_model, expert_dim), w_spec(d_model, expert_dim), w_spec(expert_dim, d_model)]
                 + [pl.BlockSpec(a.shape, lambda c, s: (0, 0), pipeline_mode=once) for a in consts],
        out_specs=pl.BlockSpec(memory_space=pl.ANY),
        out_shape=jax.ShapeDtypeStruct((n_tok, d_model), F32),
        scratch_shapes=[pltpu.VMEM(((chunk * TOP_K + row_tile) * SLOT_ROWS, LANES), jnp.uint32),
                        pltpu.VMEM((block_rows * SLOT_ROWS, LANES), jnp.uint32),
                        pltpu.VMEM((block_rows * SLOT_ROWS, LANES), F32),
                        pltpu.VMEM((block_rows * SLOT_ROWS, LANES), F32),
                        pltpu.VMEM((Y_STAGE_BUFFERS, block_rows, d_model), F32),
                        pltpu.SemaphoreType.DMA((Y_STAGE_BUFFERS,))],
        compiler_params=pltpu.CompilerParams(
            dimension_semantics=("arbitrary", "arbitrary"), vmem_limit_bytes=EXPERT_VMEM_LIMIT_BYTES),
        name="moe_experts",
    )(*[pos] * TOP_K, *[gates] * TOP_K, seg, x, wts['w_e_gate'], wts['w_e_up'], wts['w_e_down'], *consts)


def _layer_weights(l, w_in, conv_w, conv_b, conv_ln_g, conv_ln_b, w_a, pool_w, pool_scale, w_b, w_c, w_out,
                   ln1_g, ln1_b, w_router, router_bias, w_e_gate, w_e_up, w_e_down, w_s_gate, w_s_up,
                   w_s_down, ln2_g, ln2_b):
    conv_dim = conv_w.shape[-1]
    pool_dim = pool_scale.shape[-1]
    attn_dim = w_c.shape[1]
    kv_dim = N_KV_HEADS * HEAD_DIM
    o1 = 2 * conv_dim
    o2 = o1 + pool_dim
    o3 = o2 + attn_dim
    o5 = o3 + 2 * kv_dim
    wi = w_in[l]
    row = lambda v: v[l][None, :].astype(F32)
    wr_t = w_router[l].T
    wr_hi = wr_t.astype(BF16)
    return {
        'w_cv': wi[:, :o1].astype(BF16), 'w_pl': wi[:, o1:o2].astype(BF16), 'w_q': wi[:, o2:o3].astype(BF16),
        'w_kv': wi[:, o3:o5].astype(BF16), 'w_g': wi[:, o5:].astype(BF16),
        'conv_w': conv_w[l], 'conv_b': row(conv_b), 'conv_ln_g': row(conv_ln_g), 'conv_ln_b': row(conv_ln_b),
        'w_a': w_a[l].astype(BF16), 'pool_w': pool_w[l].astype(BF16), 'pool_scale': row(pool_scale),
        'w_b': w_b[l].astype(BF16), 'w_c': w_c[l].astype(BF16), 'w_out': w_out[l].astype(BF16),
        'ln1_g': row(ln1_g), 'ln1_b': row(ln1_b),
        'wr_hi': wr_hi, 'wr_lo': (wr_t - wr_hi.astype(F32)).astype(BF16),
        'router_bias': router_bias[l][:, None].astype(F32),
        'w_e_gate': w_e_gate[l].astype(BF16), 'w_e_up': w_e_up[l].astype(BF16),
        'w_e_down': w_e_down[l].astype(BF16),
        'w_s_gate': w_s_gate[l].astype(BF16), 'w_s_up': w_s_up[l].astype(BF16),
        'w_s_down': w_s_down[l].astype(BF16),
        'ln2_g': row(ln2_g), 'ln2_b': row(ln2_b),
    }


def _pick_tile(n, target):
    t = min(n, target)
    while n % t:
        t //= 2
    return t


def kernel(x_prompt, x_sample, cache_conv, cache_pool, cache_k, cache_v, w_in, conv_w, conv_b, conv_ln_g,
           conv_ln_b, w_a, pool_w, pool_scale, w_b, attn_sinks, w_c, w_out, ln1_g, ln1_b, w_router,
           router_bias, w_e_gate, w_e_up, w_e_down, w_s_gate, w_s_up, w_s_down, ln2_g, ln2_b):
    depth = w_in.shape[0]
    batch, seq, d_model = x_prompt.shape
    n_seq, n_new, _ = x_sample.shape
    win_cache = cache_k.shape[2]
    alpha = (2 * depth) ** 0.25
    tq = _pick_tile(seq, PROMPT_TILE_TOKENS)
    nb = _pick_tile(n_seq, LANES // n_new // 2)
    tm_prompt = _pick_tile(batch * seq, MOE_CHUNK_TOKENS)
    tm_sample = _pick_tile(n_seq * n_new, MOE_CHUNK_TOKENS)

    yp = x_prompt
    ys = jnp.transpose(x_sample, (1, 0, 2))
    outs = [[] for _ in range(8)]
    for l in range(depth):
        wts = _layer_weights(l, w_in, conv_w, conv_b, conv_ln_g, conv_ln_b, w_a, pool_w, pool_scale, w_b,
                             w_c, w_out, ln1_g, ln1_b, w_router, router_bias, w_e_gate, w_e_up, w_e_down,
                             w_s_gate, w_s_up, w_s_down, ln2_g, ln2_b)
        sinks = attn_sinks[l].astype(F32)

        yp, c1, p1, k1, v1 = _prompt_mixer(yp, sinks, wts, alpha, tq)
        yp = _moe(yp.reshape(batch * seq, d_model), wts, alpha, tm_prompt).reshape(batch, seq, d_model)

        ck = cache_k[l].reshape(n_seq, win_cache, LANES)
        cv = cache_v[l].reshape(n_seq, win_cache, LANES)
        ys, u2, hp2, k2, v2 = _sample_mixer(
            ys, jnp.transpose(cache_conv[l], (1, 0, 2)), jnp.transpose(cache_pool[l], (1, 0, 2)),
            ck, cv, sinks, wts, alpha, nb)
        ys = _moe(ys.reshape(n_new * n_seq, d_model), wts, alpha, tm_sample).reshape(n_new, n_seq, d_model)

        to_bl = lambda a: jnp.transpose(a, (1, 0, 2))
        outs[0].append(c1)
        outs[1].append(p1)
        outs[2].append(k1.reshape(batch, WINDOW, N_KV_HEADS, HEAD_DIM))
        outs[3].append(v1.reshape(batch, WINDOW, N_KV_HEADS, HEAD_DIM))
        outs[4].append(jnp.concatenate([cache_conv[l], to_bl(u2)], axis=1)[:, n_new:])
        outs[5].append(jnp.concatenate([cache_pool[l], to_bl(hp2)], axis=1)[:, n_new:])
        outs[6].append(jnp.concatenate([ck, to_bl(k2)], axis=1)[:, n_new:]
                       .reshape(n_seq, win_cache, N_KV_HEADS, HEAD_DIM))
        outs[7].append(jnp.concatenate([cv, to_bl(v2)], axis=1)[:, n_new:]
                       .reshape(n_seq, win_cache, N_KV_HEADS, HEAD_DIM))
    return (yp, jnp.transpose(ys, (1, 0, 2))) + tuple(jnp.stack(o) for o in outs)
```

```python
import functools
import math

import jax
import jax.numpy as jnp
from jax import lax
from jax.experimental import pallas as pl
from jax.experimental.pallas import tpu as pltpu

PAST_LEN = 16384
WINDOW = 128
BLOCK = 128
N_HEADS = 8
N_KV_HEADS = 2
GROUP = N_HEADS // N_KV_HEADS
HEAD_DIM = 64
POOL_WINDOWS = (2, 4, 8, 16)
N_EXPERT_GROUPS = 8
TOPK_GROUPS = 4
TOP_K = 8
ROUTED_SCALE = 2.5
LN_EPS = 1e-5
NEG_BIG = -1e30

LANES = 128
SUBLANES = 8
VMEM_LIMIT_BYTES = 56 * 1024 * 1024
EXPERT_VMEM_LIMIT_BYTES = 60 * 1024 * 1024
BF16_ROWS = 16
SLOT_ROWS = 4
PLAN_SUB_TOKENS = 1024
MOE_CHUNK_TOKENS = 2048
EXPERTS_PER_STEP = 4
STAGE_BUFFERS = 2
PROMPT_SEQS_PER_STEP = 2
PROMPT_TILE_TOKENS = 256
SCATTER_UNROLL = 8
COMBINE_UNROLL = 4

BF16 = jnp.bfloat16
F32 = jnp.float32


def _dot(a, b):
    return jnp.dot(a, b, preferred_element_type=F32)


def _dot_nt(a, b):
    return lax.dot_general(a, b, (((1,), (1,)), ((), ())), preferred_element_type=F32)


def _layer_norm(x, g, b):
    mu = jnp.mean(x, axis=-1, keepdims=True)
    xc = x - mu
    var = jnp.mean(xc * xc, axis=-1, keepdims=True)
    return xc * lax.rsqrt(var + LN_EPS) * g + b


def _slope(head):
    return 2.0 ** (-8.0 * (head + 1) / N_HEADS)


def _dup_lanes(x, half):
    rolled = pltpu.roll(x, HEAD_DIM, axis=1)
    lane = lax.broadcasted_iota(jnp.int32, x.shape, 1)
    lo = lane < HEAD_DIM
    if half == 0:
        return jnp.where(lo, x, rolled)
    return jnp.where(lo, rolled, x)


def _attention_group(q_pairs, kk, vv, bias_fn, valid, sinks, heads):
    rows = q_pairs[0].shape[0]
    lane = lax.broadcasted_iota(jnp.int32, (rows, LANES), 1)
    lo = lane < HEAD_DIM
    stacked = []
    for qp in q_pairs:
        stacked.append(jnp.where(lo, qp, 0.0).astype(BF16))
        stacked.append(jnp.where(lo, 0.0, qp).astype(BF16))
    lhs = jnp.concatenate(stacked, axis=0)
    s_all = _dot_nt(lhs, kk)
    probs = []
    for a, head in enumerate(heads):
        s = s_all[a * rows:(a + 1) * rows] * (HEAD_DIM ** -0.5) - bias_fn(head)
        s = jnp.where(valid, s, NEG_BIG)
        sink = sinks[head]
        m = jnp.maximum(jnp.max(s, axis=-1, keepdims=True), sink)
        e = jnp.exp(s - m)
        denom = jnp.sum(e, axis=-1, keepdims=True) + jnp.exp(sink - m)
        probs.append((e / denom).astype(BF16))
    o_all = _dot(jnp.concatenate(probs, axis=0), vv)
    outs = []
    for pair in range(GROUP // 2):
        o_lo = o_all[(2 * pair) * rows:(2 * pair + 1) * rows]
        o_hi = o_all[(2 * pair + 1) * rows:(2 * pair + 2) * rows]
        outs.append(jnp.where(lo, o_lo, o_hi))
    return outs


def _pool_mix(pooled_groups, poolw_ref, pscale_ref):
    mixed = [_dot(p.astype(BF16), poolw_ref[g]) for g, p in enumerate(pooled_groups)]
    return jnp.concatenate(mixed, axis=-1) * pscale_ref[...]


def _merge_and_norm(x, gate_logits, br_a, br_b, br_c, wout_ref, g_ref, b_ref, alpha):
    d_model = x.shape[-1]
    gates = jax.nn.sigmoid(gate_logits)
    merged = (gates[:, :d_model] * br_a + gates[:, d_model:2 * d_model] * br_b
              + gates[:, 2 * d_model:] * br_c)
    mix = _dot(merged.astype(BF16), wout_ref[...])
    return _layer_norm(alpha * x + mix, g_ref[...], b_ref[...])


def _prompt_mixer_kernel(sinks_ref, x_ref, *refs, tq, conv_hist, pool_hist, alpha):
    weights, per_seq = refs[:17], refs[17:]
    stages = [_prompt_tile_stages(pl.program_id(1), sinks_ref, x_ref.at[i], *weights,
                                  *[r.at[i] for r in per_seq],
                                  tq=tq, conv_hist=conv_hist, pool_hist=pool_hist, alpha=alpha)
              for i in range(x_ref.shape[0])]
    while stages:
        stages = [s for s in stages if next(s, False)]


def _prompt_tile_stages(t, sinks_ref, x_ref, wcv_ref, wpl_ref, wq_ref, wkv_ref, wg_ref,
                        convw_ref, convb_ref, clng_ref, clnb_ref, wa_ref,
                        poolw_ref, pscale_ref, wb_ref, wc_ref, wout_ref, ln1g_ref, ln1b_ref,
                        y_ref, oconv_ref, opool_ref, ok_ref, ov_ref,
                        ucat, pcat, kcat, vcat, *, tq, conv_hist, pool_hist, alpha):
    conv_width = convw_ref.shape[0]
    conv_dim = convw_ref.shape[1]
    pool_dim = wpl_ref.shape[1]
    gdim = pool_dim // len(POOL_WINDOWS)
    conv_state = conv_width - 1
    pool_state = max(POOL_WINDOWS) - 1

    @pl.when(t == 0)
    def _():
        ucat[0:conv_hist, :] = jnp.zeros((conv_hist, conv_dim), F32)
        pcat[0:pool_hist, :] = jnp.zeros((pool_hist, pool_dim), F32)
        kcat[0:BLOCK, :] = jnp.zeros((BLOCK, LANES), F32)
        vcat[0:BLOCK, :] = jnp.zeros((BLOCK, LANES), F32)

    x = x_ref[...]
    xb = x.astype(BF16)

    hc = _dot(xb, wcv_ref[...])
    yield True
    u = hc[:, :conv_dim] * jax.nn.sigmoid(hc[:, conv_dim:])
    ucat[conv_hist:conv_hist + tq, :] = u
    gate_logits = _dot(xb, wg_ref[...])
    yield True
    acc = jnp.zeros((tq, conv_dim), F32) + convb_ref[...]
    base = conv_hist - conv_state
    window = ucat[...]
    n_rows = window.shape[0]
    rotated = {0: window}
    for j in range(conv_width):
        start, shift = (base + j) // SUBLANES * SUBLANES, (base + j) % SUBLANES
        if shift not in rotated:
            rotated[shift] = pltpu.roll(window, n_rows - shift, axis=0)
        acc = acc + convw_ref[j:j + 1, :] * rotated[shift][start:start + tq]
    hp = _dot(xb, wpl_ref[...])
    q = _dot(xb, wq_ref[...])
    kv = _dot(xb, wkv_ref[...])
    yield True
    c = _layer_norm(acc, clng_ref[...], clnb_ref[...])
    c = c * jax.nn.sigmoid(c)
    br_a = _dot(c.astype(BF16), wa_ref[...])
    oconv_ref[...] = ucat[conv_hist + tq - conv_state:conv_hist + tq, :]
    ucat[0:conv_hist, :] = ucat[tq:tq + conv_hist, :]
    yield True

    pcat[pool_hist:pool_hist + tq, :] = hp
    pos = t * tq + lax.broadcasted_iota(jnp.int32, (tq, gdim), 0)
    pooled = []
    for g, w in enumerate(POOL_WINDOWS):
        cols = slice(g * gdim, (g + 1) * gdim)
        s = pcat[pool_hist:pool_hist + tq, cols]
        cur = s
        for j in range(1, w):
            s = s + pcat[pool_hist - j:pool_hist - j + tq, cols]
        cnt = jnp.minimum(w, pos + 1).astype(F32)
        pooled.append(s / cnt - cur)
    br_b = _dot(_pool_mix(pooled, poolw_ref, pscale_ref).astype(BF16), wb_ref[...])
    opool_ref[...] = pcat[pool_hist + tq - pool_state:pool_hist + tq, :]
    pcat[0:pool_hist, :] = pcat[tq:tq + pool_hist, :]
    yield True

    kcat[BLOCK:BLOCK + tq, :] = kv[:, :LANES]
    vcat[BLOCK:BLOCK + tq, :] = kv[:, LANES:]
    sinks = [sinks_ref[h] for h in range(N_HEADS)]
    qi = lax.broadcasted_iota(jnp.int32, (BLOCK, 2 * BLOCK), 0)
    kj = lax.broadcasted_iota(jnp.int32, (BLOCK, 2 * BLOCK), 1)
    dist_i = qi + BLOCK - kj
    dist = dist_i.astype(F32)
    in_window = (dist_i >= 0) & (dist_i <= WINDOW)
    first_key = jnp.where(t == 0, BLOCK, 0)
    att_blocks = []
    for b in range(tq // BLOCK):
        valid = in_window & (kj >= first_key) if b == 0 else in_window
        kblk = kcat[b * BLOCK:(b + 2) * BLOCK, :].astype(BF16)
        vblk = vcat[b * BLOCK:(b + 2) * BLOCK, :].astype(BF16)
        qblk = q[b * BLOCK:(b + 1) * BLOCK]
        cols = []
        for kvh in range(N_KV_HEADS):
            heads = [kvh * GROUP + a for a in range(GROUP)]
            q_pairs = [qblk[:, (kvh * GROUP + 2 * p) * HEAD_DIM:(kvh * GROUP + 2 * p + 2) * HEAD_DIM]
                       for p in range(GROUP // 2)]
            cols += _attention_group(q_pairs, _dup_lanes(kblk, kvh), _dup_lanes(vblk, kvh),
                                     lambda h: _slope(h) * dist, valid, sinks, heads)
        att_blocks.append(jnp.concatenate(cols, axis=-1))
        yield True
    att = jnp.concatenate(att_blocks, axis=0)
    br_c = _dot(att.astype(BF16), wc_ref[...])
    ok_ref[...] = kcat[tq:tq + BLOCK, :]
    ov_ref[...] = vcat[tq:tq + BLOCK, :]
    kcat[0:BLOCK, :] = kcat[tq:tq + BLOCK, :]
    vcat[0:BLOCK, :] = vcat[tq:tq + BLOCK, :]
    yield True

    y_ref[...] = _merge_and_norm(x, gate_logits, br_a, br_b, br_c, wout_ref, ln1g_ref, ln1b_ref, alpha)


def _const_spec(arr):
    nd = arr.ndim
    return pl.BlockSpec(arr.shape, lambda *_: (0,) * nd, pipeline_mode=pl.Buffered(1))


def _prompt_mixer(x, sinks, wts, alpha, tq):
    batch, seq, d_model = x.shape
    conv_width, conv_dim = wts['conv_w'].shape
    pool_dim = wts['w_pl'].shape[1]
    conv_state = conv_width - 1
    pool_state = max(POOL_WINDOWS) - 1
    conv_hist = -(-conv_state // SUBLANES) * SUBLANES
    pool_hist = -(-pool_state // SUBLANES) * SUBLANES
    group = math.gcd(batch, PROMPT_SEQS_PER_STEP)
    assert seq % tq == 0 and tq % BLOCK == 0 and tq >= max(conv_hist, pool_hist, BLOCK)
    names = ['w_cv', 'w_pl', 'w_q', 'w_kv', 'w_g', 'conv_w', 'conv_b', 'conv_ln_g', 'conv_ln_b', 'w_a',
             'pool_w', 'pool_scale', 'w_b', 'w_c', 'w_out', 'ln1_g', 'ln1_b']
    consts = [wts[n] for n in names]
    kern = functools.partial(_prompt_mixer_kernel, tq=tq, conv_hist=conv_hist, pool_hist=pool_hist,
                             alpha=alpha)
    out_shapes = (
        jax.ShapeDtypeStruct((batch, seq, d_model), F32),
        jax.ShapeDtypeStruct((batch, conv_state, conv_dim), F32),
        jax.ShapeDtypeStruct((batch, pool_state, pool_dim), F32),
        jax.ShapeDtypeStruct((batch, WINDOW, LANES), F32),
        jax.ShapeDtypeStruct((batch, WINDOW, LANES), F32),
    )
    return pl.pallas_call(
        kern,
        grid=(batch // group, seq // tq),
        in_specs=[pl.BlockSpec(memory_space=pltpu.SMEM),
                  pl.BlockSpec((group, tq, d_model), lambda b, t: (b, t, 0))]
                 + [_const_spec(a) for a in consts],
        out_specs=(
            pl.BlockSpec((group, tq, d_model), lambda b, t: (b, t, 0)),
            pl.BlockSpec((group, conv_state, conv_dim), lambda b, t: (b, 0, 0)),
            pl.BlockSpec((group, pool_state, pool_dim), lambda b, t: (b, 0, 0)),
            pl.BlockSpec((group, WINDOW, LANES), lambda b, t: (b, 0, 0)),
            pl.BlockSpec((group, WINDOW, LANES), lambda b, t: (b, 0, 0)),
        ),
        out_shape=out_shapes,
        scratch_shapes=[
            pltpu.VMEM((group, conv_hist + tq, conv_dim), F32),
            pltpu.VMEM((group, pool_hist + tq, pool_dim), F32),
            pltpu.VMEM((group, BLOCK + tq, LANES), F32),
            pltpu.VMEM((group, BLOCK + tq, LANES), F32),
        ],
        compiler_params=pltpu.CompilerParams(
            dimension_semantics=("arbitrary", "arbitrary"), vmem_limit_bytes=VMEM_LIMIT_BYTES),
        name="prompt_mixer",
    )(sinks, x, *consts)


def _sample_mixer_kernel(sinks_ref, x_ref, cconv_ref, cpool_ref, ck_ref, cv_ref,
                         wcv_ref, wpl_ref, wq_ref, wkv_ref, wg_ref,
                         convw_ref, convb_ref, clng_ref, clnb_ref, wa_ref,
                         poolw_ref, pscale_ref, wb_ref, wc_ref, wout_ref, ln1g_ref, ln1b_ref,
                         y_ref, ou_ref, op_ref, ok_ref, ov_ref, *, alpha):
    n_new, nb, d_model = x_ref.shape
    conv_width, conv_dim = convw_ref.shape
    conv_state = conv_width - 1
    pool_dim = wpl_ref.shape[1]
    gdim = pool_dim // len(POOL_WINDOWS)
    pool_state = max(POOL_WINDOWS) - 1
    win_cache = ck_ref.shape[1]
    rows = n_new * nb

    x = x_ref[...].reshape(rows, d_model)
    xb = x.astype(BF16)

    hc = _dot(xb, wcv_ref[...])
    u = hc[:, :conv_dim] * jax.nn.sigmoid(hc[:, conv_dim:])
    u_l = [u[l * nb:(l + 1) * nb] for l in range(n_new)]
    conv_rows = []
    for l in range(n_new):
        acc = jnp.zeros((nb, conv_dim), F32) + convb_ref[...]
        for j in range(conv_width):
            i = l + j
            src = cconv_ref[i] if i < conv_state else u_l[i - conv_state]
            acc = acc + convw_ref[j:j + 1, :] * src
        conv_rows.append(acc)
    c = _layer_norm(jnp.concatenate(conv_rows, axis=0), clng_ref[...], clnb_ref[...])
    c = c * jax.nn.sigmoid(c)
    br_a = _dot(c.astype(BF16), wa_ref[...])
    ou_ref[...] = u.reshape(n_new, nb, conv_dim)

    hp = _dot(xb, wpl_ref[...])
    hp_l = [hp[l * nb:(l + 1) * nb] for l in range(n_new)]
    pooled_rows = []
    for l in range(n_new):
        groups = []
        for g, w in enumerate(POOL_WINDOWS):
            cols = slice(g * gdim, (g + 1) * gdim)
            s = hp_l[l][:, cols]
            for j in range(1, w):
                i = l - j
                s = s + (hp_l[i][:, cols] if i >= 0 else cpool_ref[pool_state + i][:, cols])
            cnt = float(min(w, PAST_LEN + l + 1))
            groups.append(s / cnt - hp_l[l][:, cols])
        pooled_rows.append(groups)
    pooled = [jnp.concatenate([pooled_rows[l][g] for l in range(n_new)], axis=0)
              for g in range(len(POOL_WINDOWS))]
    br_b = _dot(_pool_mix(pooled, poolw_ref, pscale_ref).astype(BF16), wb_ref[...])
    op_ref[...] = hp.reshape(n_new, nb, pool_dim)

    q = _dot(xb, wq_ref[...])
    kv = _dot(xb, wkv_ref[...])
    k_new = kv[:, :LANES]
    v_new = kv[:, LANES:]
    ok_ref[...] = k_new.reshape(n_new, nb, LANES)
    ov_ref[...] = v_new.reshape(n_new, nb, LANES)
    n_cached = nb * win_cache
    pad = jnp.zeros((LANES - rows, LANES), BF16)
    k_all = jnp.concatenate([ck_ref[...].reshape(n_cached, LANES).astype(BF16), k_new.astype(BF16), pad], axis=0)
    v_all = jnp.concatenate([cv_ref[...].reshape(n_cached, LANES).astype(BF16), v_new.astype(BF16), pad], axis=0)
    n_keys = n_cached + LANES
    r = lax.broadcasted_iota(jnp.int32, (rows, n_keys), 0)
    col = lax.broadcasted_iota(jnp.int32, (rows, n_keys), 1)
    q_l = r // nb
    q_b = r % nb
    cached = col < n_cached
    cn = col - n_cached
    k_b = jnp.where(cached, col // win_cache, cn % nb)
    dist_i = jnp.where(cached, q_l + win_cache - col % win_cache, q_l - cn // nb)
    valid = (k_b == q_b) & (dist_i >= 0) & (dist_i <= WINDOW) & (cached | (cn < rows))
    dist = dist_i.astype(F32)
    sinks = [sinks_ref[h] for h in range(N_HEADS)]
    cols = []
    for kvh in range(N_KV_HEADS):
        heads = [kvh * GROUP + a for a in range(GROUP)]
        q_pairs = [q[:, (kvh * GROUP + 2 * p) * HEAD_DIM:(kvh * GROUP + 2 * p + 2) * HEAD_DIM]
                   for p in range(GROUP // 2)]
        cols += _attention_group(q_pairs, _dup_lanes(k_all, kvh), _dup_lanes(v_all, kvh),
                                 lambda h: _slope(h) * dist, valid, sinks, heads)
    att = jnp.concatenate(cols, axis=-1)
    br_c = _dot(att.astype(BF16), wc_ref[...])

    y = _merge_and_norm(x, _dot(xb, wg_ref[...]), br_a, br_b, br_c, wout_ref, ln1g_ref, ln1b_ref, alpha)
    y_ref[...] = y.reshape(n_new, nb, d_model)


def _sample_mixer(x_lb, cconv_t, cpool_t, ck, cv, sinks, wts, alpha, nb):
    n_new, n_seq, d_model = x_lb.shape
    conv_state, _, conv_dim = cconv_t.shape
    pool_state, _, pool_dim = cpool_t.shape
    win_cache = ck.shape[1]
    assert n_seq % nb == 0 and nb % SUBLANES == 0 and n_new * nb <= LANES
    names = ['w_cv', 'w_pl', 'w_q', 'w_kv', 'w_g', 'conv_w', 'conv_b', 'conv_ln_g', 'conv_ln_b', 'w_a',
             'pool_w', 'pool_scale', 'w_b', 'w_c', 'w_out', 'ln1_g', 'ln1_b']
    consts = [wts[n] for n in names]

    def lb_spec(rows, width):
        return pl.BlockSpec((rows, nb, width), lambda c: (0, c, 0))

    out_shapes = (
        jax.ShapeDtypeStruct((n_new, n_seq, d_model), F32),
        jax.ShapeDtypeStruct((n_new, n_seq, conv_dim), F32),
        jax.ShapeDtypeStruct((n_new, n_seq, pool_dim), F32),
        jax.ShapeDtypeStruct((n_new, n_seq, LANES), F32),
        jax.ShapeDtypeStruct((n_new, n_seq, LANES), F32),
    )
    return pl.pallas_call(
        functools.partial(_sample_mixer_kernel, alpha=alpha),
        grid=(n_seq // nb,),
        in_specs=[pl.BlockSpec(memory_space=pltpu.SMEM),
                  lb_spec(n_new, d_model), lb_spec(conv_state, conv_dim), lb_spec(pool_state, pool_dim),
                  pl.BlockSpec((nb, win_cache, LANES), lambda c: (c, 0, 0)),
                  pl.BlockSpec((nb, win_cache, LANES), lambda c: (c, 0, 0))]
                 + [_const_spec(a) for a in consts],
        out_specs=(lb_spec(n_new, d_model), lb_spec(n_new, conv_dim), lb_spec(n_new, pool_dim),
                   lb_spec(n_new, LANES), lb_spec(n_new, LANES)),
        out_shape=out_shapes,
        compiler_params=pltpu.CompilerParams(
            dimension_semantics=("arbitrary",), vmem_limit_bytes=VMEM_LIMIT_BYTES),
        name="sample_mixer",
    )(sinks, x_lb, cconv_t, cpool_t, ck, cv, *consts)


def _route(x, wr_hi_ref, wr_lo_ref, rbias_ref):
    n_experts = wr_hi_ref.shape[0]
    per_group = n_experts // N_EXPERT_GROUPS
    tm = x.shape[0]
    x_hi = x.astype(BF16)
    x_lo = (x - x_hi.astype(F32)).astype(BF16)
    logits = (_dot_nt(wr_hi_ref[...], x_hi) + _dot_nt(wr_lo_ref[...], x_hi)
              + _dot_nt(wr_hi_ref[...], x_lo))
    scores = jax.nn.sigmoid(logits)
    sel = scores + rbias_ref[...]
    sub = lax.broadcasted_iota(jnp.int32, (per_group, tm), 0)
    grp_scores = []
    for g in range(N_EXPERT_GROUPS):
        v = sel[g * per_group:(g + 1) * per_group]
        m1 = jnp.max(v, axis=0, keepdims=True)
        first = jnp.min(jnp.where(v == m1, sub, per_group), axis=0, keepdims=True)
        m2 = jnp.max(jnp.where(sub == first, -jnp.inf, v), axis=0, keepdims=True)
        grp_scores.append(m1 + m2)
    keep = []
    for g in range(N_EXPERT_GROUPS):
        rank = jnp.zeros((1, tm), jnp.int32)
        for o in range(N_EXPERT_GROUPS):
            if o == g:
                continue
            ahead = (grp_scores[o] >= grp_scores[g]) if o < g else (grp_scores[o] > grp_scores[g])
            rank = rank + ahead.astype(jnp.int32)
        keep.append(jnp.broadcast_to(rank, (per_group, tm)))
    cand = jnp.where(jnp.concatenate(keep, axis=0) < TOPK_GROUPS, sel, -jnp.inf)
    eidx = lax.broadcasted_iota(jnp.int32, (n_experts, tm), 0)
    chosen = jnp.zeros((n_experts, tm), F32)
    picks, weights = [], []
    for _ in range(TOP_K):
        m = jnp.max(cand, axis=0, keepdims=True)
        first = jnp.min(jnp.where(cand == m, eidx, n_experts), axis=0, keepdims=True)
        pick = eidx == first
        picks.append(first)
        weights.append(jnp.sum(jnp.where(pick, scores, 0.0), axis=0, keepdims=True))
        chosen = jnp.where(pick, 1.0, chosen)
        cand = jnp.where(pick, -jnp.inf, cand)
    total = weights[0]
    for w in weights[1:]:
        total = total + w
    gates = [w / total * ROUTED_SCALE for w in weights]
    return picks, gates, chosen


def _plan_kernel(x_ref, wrh_ref, wrl_ref, rbias_ref, gates_ref, pos_ref, cnt_ref, off_ref):
    n_experts = wrh_ref.shape[0]
    tm = x_ref.shape[0]
    sub = min(tm, PLAN_SUB_TOKENS)
    picks, gates, chosen = _route(x_ref[...], wrh_ref, wrl_ref, rbias_ref)
    chosen_b = chosen.astype(BF16)
    before = (lax.broadcasted_iota(jnp.int32, (sub, sub), 0)
              < lax.broadcasted_iota(jnp.int32, (sub, sub), 1)).astype(BF16)
    ranks = []
    cnt = jnp.zeros((n_experts, 1), F32)
    for b in range(tm // sub):
        ranks.append(_dot(chosen_b[:, b * sub:(b + 1) * sub], before) + cnt)
        cnt = cnt + jnp.sum(chosen[:, b * sub:(b + 1) * sub], axis=1, keepdims=True)
    rank_in_expert = jnp.concatenate(ranks, axis=1)
    lower = (lax.broadcasted_iota(jnp.int32, (n_experts, n_experts), 1)
             < lax.broadcasted_iota(jnp.int32, (n_experts, n_experts), 0)).astype(BF16)
    off = jnp.sum(_dot(lower, chosen_b), axis=1, keepdims=True)
    rank = off + rank_in_expert
    eidx = lax.broadcasted_iota(jnp.int32, (n_experts, tm), 0)
    pos = [jnp.sum(jnp.where(eidx == p, rank, 0.0), axis=0, keepdims=True) for p in picks]
    for k in range(TOP_K):
        gates_ref[k] = gates[k]
        pos_ref[k] = (pos[k] * SLOT_ROWS).astype(jnp.int32)
    cnt_ref[0] = jnp.broadcast_to(cnt, (n_experts, LANES)).astype(jnp.int32)
    off_ref[0] = jnp.broadcast_to(off, (n_experts, LANES)).astype(jnp.int32)


def _moe_plan(x, wts, chunk):
    n_tok, d_model = x.shape
    n_experts = wts['wr_hi'].shape[0]
    n_chunks = n_tok // chunk
    consts = [wts['wr_hi'], wts['wr_lo'], wts['router_bias']]
    gates, pos, cnt, off = pl.pallas_call(
        _plan_kernel,
        grid=(n_chunks,),
        in_specs=[pl.BlockSpec((chunk, d_model), lambda c: (c, 0))] + [_const_spec(a) for a in consts],
        out_specs=(pl.BlockSpec((TOP_K, 1, chunk), lambda c: (0, 0, c)),
                   pl.BlockSpec((TOP_K, 1, chunk), lambda c: (0, 0, c)),
                   pl.BlockSpec((1, n_experts, LANES), lambda c: (c, 0, 0)),
                   pl.BlockSpec((1, n_experts, LANES), lambda c: (c, 0, 0))),
        out_shape=(jax.ShapeDtypeStruct((TOP_K, 1, n_tok), F32),
                   jax.ShapeDtypeStruct((TOP_K, 1, n_tok), jnp.int32),
                   jax.ShapeDtypeStruct((n_chunks, n_experts, LANES), jnp.int32),
                   jax.ShapeDtypeStruct((n_chunks, n_experts, LANES), jnp.int32)),
        compiler_params=pltpu.CompilerParams(
            dimension_semantics=("arbitrary",), vmem_limit_bytes=VMEM_LIMIT_BYTES),
        name="moe_plan",
    )(x, *consts)
    seg = jnp.concatenate([cnt[:, :, 0], off[:, :, 0]], axis=1)
    return gates, pos, seg.reshape(-1)


def _expert_kernel(*refs, tokens, row_tile, experts_per_step, block_rows, alpha):
    pos_refs, gate_refs = refs[:TOP_K], refs[TOP_K:2 * TOP_K]
    (seg_ref, x_hbm, weg_ref, weu_ref, wed_ref, wsg_ref, wsu_ref, wsd_ref, g_ref, b_ref,
     y_hbm, sbuf, pbuf, top_buf, bot_buf, x_stage, x_sem, y_stage, y_sem) = refs[2 * TOP_K:]
    step = pl.program_id(1)
    d_model = x_hbm.shape[1]
    half = d_model // 2
    n_experts = seg_ref.shape[0] // 2
    pairs = tokens * TOP_K
    n_blocks = tokens // block_rows
    chunk_row = pl.program_id(0) * tokens

    def hbm_rows(r):
        return pl.ds(pl.multiple_of(chunk_row + r * block_rows, block_rows), block_rows)

    def x_copy(r, buf):
        return pltpu.make_async_copy(x_hbm.at[hbm_rows(r), :], x_stage.at[buf], x_sem.at[buf])

    def staged_x(r):
        buf = lax.rem(r, STAGE_BUFFERS)
        x_copy(r, buf).wait()

        @pl.when(r + 1 < n_blocks)
        def _():
            x_copy(r + 1, lax.rem(r + 1, STAGE_BUFFERS)).start()
        return x_stage.at[buf]

    def pack(a, b):
        return pltpu.pack_elementwise([a, b], packed_dtype=BF16)

    def unpack(w):
        return tuple(pltpu.unpack_elementwise(w, index=i, packed_dtype=BF16, unpacked_dtype=F32)
                     for i in range(2))

    def slot_at(ref, row):
        return ref.at[pl.ds(pl.multiple_of(row, SLOT_ROWS), SLOT_ROWS), :]

    def slabs(ref, first_row, rows):
        return [ref.at[pl.ds(first_row + j, rows, stride=SLOT_ROWS), :] for j in range(SLOT_ROWS)]

    @pl.when(step == 0)
    def _():
        sbuf[pairs * SLOT_ROWS:, :] = jnp.zeros((row_tile * SLOT_ROWS, LANES), jnp.uint32)
        x_copy(0, 0).start()

        def scatter_block(r, carry):
            x_blk = staged_x(r)
            words = pack(x_blk[:, :half], x_blk[:, half:])
            for j, slab in enumerate(slabs(pbuf, 0, block_rows)):
                slab[...] = words[:, j * LANES:(j + 1) * LANES]

            def scatter(i, carry):
                for u in range(SCATTER_UNROLL):
                    local = i * SCATTER_UNROLL + u
                    t = r * block_rows + local
                    slot = slot_at(pbuf, local * SLOT_ROWS)[...]
                    for k in range(TOP_K):
                        slot_at(sbuf, pos_refs[k][0, 0, t])[...] = slot
                return carry
            return lax.fori_loop(0, block_rows // SCATTER_UNROLL, scatter, carry)
        lax.fori_loop(0, n_blocks, scatter_block, 0)

    def unpack_rows(words):
        return jnp.concatenate(unpack(words), axis=-1)

    def hidden(words, j):
        lb = unpack_rows(words).astype(BF16)
        return (jax.nn.silu(_dot(lb, weg_ref[j])) * _dot(lb, weu_ref[j])).astype(BF16)

    def expert_rows(words, j):
        return _dot(hidden(words, j), wed_ref[j])

    def load_tile(base):
        return jnp.concatenate([s[...] for s in slabs(sbuf, base, row_tile)], axis=-1)

    def store_tile(base, words, y, n_live):
        live = lax.broadcasted_iota(jnp.int32, y.shape, 0) < n_live
        out = jnp.where(live, y, unpack_rows(words))
        new_words = pack(out[:, :half], out[:, half:])
        for j, slab in enumerate(slabs(sbuf, base, row_tile)):
            slab[...] = new_words[:, j * LANES:(j + 1) * LANES]

    experts = [step * experts_per_step + j for j in range(experts_per_step)]
    n_rows = [seg_ref[ex] for ex in experts]
    bases = [pl.multiple_of(seg_ref[n_experts + ex] * SLOT_ROWS, SLOT_ROWS) for ex in experts]
    lhs = [load_tile(b) for b in bases]
    hs = [hidden(l, j) for j, l in enumerate(lhs)]
    ys = [_dot(h, wed_ref[j]) for j, h in enumerate(hs)]
    for j in range(experts_per_step):
        store_tile(bases[j], lhs[j], ys[j], n_rows[j])

    for j in range(experts_per_step):
        def run_tile(i, carry, j=j):
            base = pl.multiple_of(bases[j] + i * (row_tile * SLOT_ROWS), SLOT_ROWS)
            tile = load_tile(base)
            store_tile(base, tile, expert_rows(tile, j), n_rows[j] - i * row_tile)
            return carry
        lax.fori_loop(1, (n_rows[j] + row_tile - 1) // row_tile, run_tile, 0)

    @pl.when(step == pl.num_programs(1) - 1)
    def _():
        def y_copy(r, buf):
            return pltpu.make_async_copy(y_stage.at[buf], y_hbm.at[hbm_rows(r), :], y_sem.at[buf])

        x_copy(0, 0).start()

        def finish(r, carry):
            buf = lax.rem(r, STAGE_BUFFERS)

            @pl.when(r >= STAGE_BUFFERS)
            def _():
                y_copy(r - STAGE_BUFFERS, buf).wait()

            def combine(i, carry):
                for u in range(COMBINE_UNROLL):
                    local = i * COMBINE_UNROLL + u
                    t = r * block_rows + local
                    top = jnp.zeros((SLOT_ROWS, LANES), F32)
                    bot = jnp.zeros((SLOT_ROWS, LANES), F32)
                    for k in range(TOP_K):
                        a, b = unpack(slot_at(sbuf, pos_refs[k][0, 0, t])[...])
                        gate = gate_refs[k][0, 0, t]
                        top = top + gate * a
                        bot = bot + gate * b
                    slot_at(top_buf, local * SLOT_ROWS)[...] = top
                    slot_at(bot_buf, local * SLOT_ROWS)[...] = bot
                return carry
            lax.fori_loop(0, block_rows // COMBINE_UNROLL, combine, 0)

            routed = jnp.concatenate([s[...] for half_buf in (top_buf, bot_buf)
                                      for s in slabs(half_buf, 0, block_rows)], axis=-1)
            x = staged_x(r)[...]
            xb = x.astype(BF16)
            hs = jax.nn.silu(_dot(xb, wsg_ref[...])) * _dot(xb, wsu_ref[...])
            shared = _dot(hs.astype(BF16), wsd_ref[...])
            y_stage[buf] = _layer_norm(alpha * x + (routed + shared), g_ref[...], b_ref[...])
            y_copy(r, buf).start()
            return carry
        lax.fori_loop(0, n_blocks, finish, 0)
        for r in range(max(n_blocks - STAGE_BUFFERS, 0), n_blocks):
            y_copy(r, r % STAGE_BUFFERS).wait()


def _expert_row_tile(chunk, n_experts):
    mean = chunk * TOP_K / n_experts
    rows = mean + 3.0 * math.sqrt(mean * (1.0 - TOP_K / n_experts))
    return int(-(-rows // BF16_ROWS) * BF16_ROWS)


def _moe(x, wts, alpha, chunk):
    n_tok, d_model = x.shape
    n_experts, _, expert_dim = wts['w_e_gate'].shape
    assert d_model == 2 * SLOT_ROWS * LANES and n_tok % chunk == 0 and n_experts % EXPERTS_PER_STEP == 0
    gates, pos, seg = _moe_plan(x, wts, chunk)
    row_tile = _expert_row_tile(chunk, n_experts)
    block_rows = _pick_tile(chunk, 256)
    consts = [wts['w_s_gate'], wts['w_s_up'], wts['w_s_down'], wts['ln2_g'], wts['ln2_b']]
    once = pl.Buffered(1)
    kern = functools.partial(_expert_kernel, tokens=chunk, row_tile=row_tile,
                             experts_per_step=EXPERTS_PER_STEP, block_rows=block_rows, alpha=alpha)

    def w_spec(rows, cols):
        return pl.BlockSpec((EXPERTS_PER_STEP, rows, cols), lambda c, s: (s, 0, 0))

    def pick_spec(k):
        return pl.BlockSpec((1, 1, chunk), lambda c, s: (k, 0, c), memory_space=pltpu.SMEM, pipeline_mode=once)

    picks = [pick_spec(k) for k in range(TOP_K)]
    return pl.pallas_call(
        kern,
        grid=(n_tok // chunk, n_experts // EXPERTS_PER_STEP),
        in_specs=picks + picks + [
                  pl.BlockSpec((2 * n_experts,), lambda c, s: (c,), memory_space=pltpu.SMEM, pipeline_mode=once),
                  pl.BlockSpec(memory_space=pl.ANY),
                  w_spec(d_model, expert_dim), w_spec(d_model, expert_dim), w_spec(expert_dim, d_model)]
                 + [pl.BlockSpec(a.shape, lambda c, s: (0, 0), pipeline_mode=once) for a in consts],
        out_specs=pl.BlockSpec(memory_space=pl.ANY),
        out_shape=jax.ShapeDtypeStruct((n_tok, d_model), F32),
        scratch_shapes=[pltpu.VMEM(((chunk * TOP_K + row_tile) * SLOT_ROWS, LANES), jnp.uint32),
                        pltpu.VMEM((block_rows * SLOT_ROWS, LANES), jnp.uint32),
                        pltpu.VMEM((block_rows * SLOT_ROWS, LANES), F32),
                        pltpu.VMEM((block_rows * SLOT_ROWS, LANES), F32),
                        pltpu.VMEM((STAGE_BUFFERS, block_rows, d_model), F32),
                        pltpu.SemaphoreType.DMA((STAGE_BUFFERS,)),
                        pltpu.VMEM((STAGE_BUFFERS, block_rows, d_model), F32),
                        pltpu.SemaphoreType.DMA((STAGE_BUFFERS,))],
        compiler_params=pltpu.CompilerParams(
            dimension_semantics=("arbitrary", "arbitrary"), vmem_limit_bytes=EXPERT_VMEM_LIMIT_BYTES),
        name="moe_experts",
    )(*[pos] * TOP_K, *[gates] * TOP_K, seg, x, wts['w_e_gate'], wts['w_e_up'], wts['w_e_down'], *consts)


def _layer_weights(l, w_in, conv_w, conv_b, conv_ln_g, conv_ln_b, w_a, pool_w, pool_scale, w_b, w_c, w_out,
                   ln1_g, ln1_b, w_router, router_bias, w_e_gate, w_e_up, w_e_down, w_s_gate, w_s_up,
                   w_s_down, ln2_g, ln2_b):
    conv_dim = conv_w.shape[-1]
    pool_dim = pool_scale.shape[-1]
    attn_dim = w_c.shape[1]
    kv_dim = N_KV_HEADS * HEAD_DIM
    o1 = 2 * conv_dim
    o2 = o1 + pool_dim
    o3 = o2 + attn_dim
    o5 = o3 + 2 * kv_dim
    wi = w_in[l]
    row = lambda v: v[l][None, :].astype(F32)
    wr_t = w_router[l].T
    wr_hi = wr_t.astype(BF16)
    return {
        'w_cv': wi[:, :o1].astype(BF16), 'w_pl': wi[:, o1:o2].astype(BF16), 'w_q': wi[:, o2:o3].astype(BF16),
        'w_kv': wi[:, o3:o5].astype(BF16), 'w_g': wi[:, o5:].astype(BF16),
        'conv_w': conv_w[l], 'conv_b': row(conv_b), 'conv_ln_g': row(conv_ln_g), 'conv_ln_b': row(conv_ln_b),
        'w_a': w_a[l].astype(BF16), 'pool_w': pool_w[l].astype(BF16), 'pool_scale': row(pool_scale),
        'w_b': w_b[l].astype(BF16), 'w_c': w_c[l].astype(BF16), 'w_out': w_out[l].astype(BF16),
        'ln1_g': row(ln1_g), 'ln1_b': row(ln1_b),
        'wr_hi': wr_hi, 'wr_lo': (wr_t - wr_hi.astype(F32)).astype(BF16),
        'router_bias': router_bias[l][:, None].astype(F32),
        'w_e_gate': w_e_gate[l].astype(BF16), 'w_e_up': w_e_up[l].astype(BF16),
        'w_e_down': w_e_down[l].astype(BF16),
        'w_s_gate': w_s_gate[l].astype(BF16), 'w_s_up': w_s_up[l].astype(BF16),
        'w_s_down': w_s_down[l].astype(BF16),
        'ln2_g': row(ln2_g), 'ln2_b': row(ln2_b),
    }


def _pick_tile(n, target):
    t = min(n, target)
    while n % t:
        t //= 2
    return t


def kernel(x_prompt, x_sample, cache_conv, cache_pool, cache_k, cache_v, w_in, conv_w, conv_b, conv_ln_g,
           conv_ln_b, w_a, pool_w, pool_scale, w_b, attn_sinks, w_c, w_out, ln1_g, ln1_b, w_router,
           router_bias, w_e_gate, w_e_up, w_e_down, w_s_gate, w_s_up, w_s_down, ln2_g, ln2_b):
    depth = w_in.shape[0]
    batch, seq, d_model = x_prompt.shape
    n_seq, n_new, _ = x_sample.shape
    win_cache = cache_k.shape[2]
    alpha = (2 * depth) ** 0.25
    tq = _pick_tile(seq, PROMPT_TILE_TOKENS)
    nb = _pick_tile(n_seq, LANES // n_new // 2)
    tm_prompt = _pick_tile(batch * seq, MOE_CHUNK_TOKENS)
    tm_sample = _pick_tile(n_seq * n_new, MOE_CHUNK_TOKENS)

    yp = x_prompt
    ys = jnp.transpose(x_sample, (1, 0, 2))
    outs = [[] for _ in range(8)]
    for l in range(depth):
        wts = _layer_weights(l, w_in, conv_w, conv_b, conv_ln_g, conv_ln_b, w_a, pool_w, pool_scale, w_b,
                             w_c, w_out, ln1_g, ln1_b, w_router, router_bias, w_e_gate, w_e_up, w_e_down,
                             w_s_gate, w_s_up, w_s_down, ln2_g, ln2_b)
        sinks = attn_sinks[l].astype(F32)

        yp, c1, p1, k1, v1 = _prompt_mixer(yp, sinks, wts, alpha, tq)
        yp = _moe(yp.reshape(batch * seq, d_model), wts, alpha, tm_prompt).reshape(batch, seq, d_model)

        ck = cache_k[l].reshape(n_seq, win_cache, LANES)
        cv = cache_v[l].reshape(n_seq, win_cache, LANES)
        ys, u2, hp2, k2, v2 = _sample_mixer(
            ys, jnp.transpose(cache_conv[l], (1, 0, 2)), jnp.transpose(cache_pool[l], (1, 0, 2)),
            ck, cv, sinks, wts, alpha, nb)
        ys = _moe(ys.reshape(n_new * n_seq, d_model), wts, alpha, tm_sample).reshape(n_new, n_seq, d_model)

        to_bl = lambda a: jnp.transpose(a, (1, 0, 2))
        outs[0].append(c1)
        outs[1].append(p1)
        outs[2].append(k1.reshape(batch, WINDOW, N_KV_HEADS, HEAD_DIM))
        outs[3].append(v1.reshape(batch, WINDOW, N_KV_HEADS, HEAD_DIM))
        outs[4].append(jnp.concatenate([cache_conv[l], to_bl(u2)], axis=1)[:, n_new:])
        outs[5].append(jnp.concatenate([cache_pool[l], to_bl(hp2)], axis=1)[:, n_new:])
        outs[6].append(jnp.concatenate([ck, to_bl(k2)], axis=1)[:, n_new:]
                       .reshape(n_seq, win_cache, N_KV_HEADS, HEAD_DIM))
        outs[7].append(jnp.concatenate([cv, to_bl(v2)], axis=1)[:, n_new:]
                       .reshape(n_seq, win_cache, N_KV_HEADS, HEAD_DIM))
    return (yp, jnp.transpose(ys, (1, 0, 2))) + tuple(jnp.stack(o) for o in outs)
```

```python
import functools
import math

import jax
import jax.numpy as jnp
from jax import lax
from jax.experimental import pallas as pl
from jax.experimental.pallas import tpu as pltpu

PAST_LEN = 16384
WINDOW = 128
BLOCK = 128
N_HEADS = 8
N_KV_HEADS = 2
GROUP = N_HEADS // N_KV_HEADS
HEAD_DIM = 64
POOL_WINDOWS = (2, 4, 8, 16)
N_EXPERT_GROUPS = 8
TOPK_GROUPS = 4
TOP_K = 8
ROUTED_SCALE = 2.5
LN_EPS = 1e-5
NEG_BIG = -1e30

LANES = 128
SUBLANES = 8
VMEM_LIMIT_BYTES = 56 * 1024 * 1024
EXPERT_VMEM_LIMIT_BYTES = 60 * 1024 * 1024
BF16_ROWS = 16
SLOT_ROWS = 4
PLAN_SUB_TOKENS = 1024
MOE_CHUNK_TOKENS = 2048
EXPERTS_PER_STEP = 4
STAGE_BUFFERS = 2
PROMPT_SEQS_PER_STEP = 2
PROMPT_TILE_TOKENS = 256
SCATTER_UNROLL = 8
COMBINE_UNROLL = 4

BF16 = jnp.bfloat16
F32 = jnp.float32


def _dot(a, b):
    return jnp.dot(a, b, preferred_element_type=F32)


def _dot_nt(a, b):
    return lax.dot_general(a, b, (((1,), (1,)), ((), ())), preferred_element_type=F32)


def _layer_norm(x, g, b):
    mu = jnp.mean(x, axis=-1, keepdims=True)
    xc = x - mu
    var = jnp.mean(xc * xc, axis=-1, keepdims=True)
    return xc * lax.rsqrt(var + LN_EPS) * g + b


def _slope(head):
    return 2.0 ** (-8.0 * (head + 1) / N_HEADS)


def _dup_lanes(x, half):
    rolled = pltpu.roll(x, HEAD_DIM, axis=1)
    lane = lax.broadcasted_iota(jnp.int32, x.shape, 1)
    lo = lane < HEAD_DIM
    if half == 0:
        return jnp.where(lo, x, rolled)
    return jnp.where(lo, rolled, x)


def _attention_group(q_pairs, kk, vv, bias_fn, valid, sinks, heads):
    rows = q_pairs[0].shape[0]
    lane = lax.broadcasted_iota(jnp.int32, (rows, LANES), 1)
    lo = lane < HEAD_DIM
    stacked = []
    for qp in q_pairs:
        stacked.append(jnp.where(lo, qp, 0.0).astype(BF16))
        stacked.append(jnp.where(lo, 0.0, qp).astype(BF16))
    lhs = jnp.concatenate(stacked, axis=0)
    s_all = _dot_nt(lhs, kk)
    probs = []
    for a, head in enumerate(heads):
        s = s_all[a * rows:(a + 1) * rows] * (HEAD_DIM ** -0.5) - bias_fn(head)
        if valid is not None:
            s = jnp.where(valid, s, NEG_BIG)
        sink = sinks[head]
        m = jnp.maximum(jnp.max(s, axis=-1, keepdims=True), sink)
        e = jnp.exp(s - m)
        denom = jnp.sum(e, axis=-1, keepdims=True) + jnp.exp(sink - m)
        probs.append((e / denom).astype(BF16))
    o_all = _dot(jnp.concatenate(probs, axis=0), vv)
    outs = []
    for pair in range(GROUP // 2):
        o_lo = o_all[(2 * pair) * rows:(2 * pair + 1) * rows]
        o_hi = o_all[(2 * pair + 1) * rows:(2 * pair + 2) * rows]
        outs.append(jnp.where(lo, o_lo, o_hi))
    return outs


def _pool_mix(pooled_groups, poolw_ref, pscale_ref):
    mixed = [_dot(p.astype(BF16), poolw_ref[g]) for g, p in enumerate(pooled_groups)]
    return jnp.concatenate(mixed, axis=-1) * pscale_ref[...]


def _merge_and_norm(x, gate_logits, br_a, br_b, br_c, wout_ref, g_ref, b_ref, alpha):
    d_model = x.shape[-1]
    gates = jax.nn.sigmoid(gate_logits)
    merged = (gates[:, :d_model] * br_a + gates[:, d_model:2 * d_model] * br_b
              + gates[:, 2 * d_model:] * br_c)
    mix = _dot(merged.astype(BF16), wout_ref[...])
    return _layer_norm(alpha * x + mix, g_ref[...], b_ref[...])


def _prompt_mixer_kernel(sinks_ref, x_ref, *refs, tq, conv_hist, pool_hist, alpha):
    weights, per_seq = refs[:17], refs[17:]
    stages = [_prompt_tile_stages(pl.program_id(1), sinks_ref, x_ref.at[i], *weights,
                                  *[r.at[i] for r in per_seq],
                                  tq=tq, conv_hist=conv_hist, pool_hist=pool_hist, alpha=alpha)
              for i in range(x_ref.shape[0])]
    while stages:
        stages = [s for s in stages if next(s, False)]


def _prompt_tile_stages(t, sinks_ref, x_ref, wcv_ref, wpl_ref, wq_ref, wkv_ref, wg_ref,
                        convw_ref, convb_ref, clng_ref, clnb_ref, wa_ref,
                        poolw_ref, pscale_ref, wb_ref, wc_ref, wout_ref, ln1g_ref, ln1b_ref,
                        y_ref, oconv_ref, opool_ref, ok_ref, ov_ref,
                        ucat, pcat, kcat, vcat, *, tq, conv_hist, pool_hist, alpha):
    conv_width = convw_ref.shape[0]
    conv_dim = convw_ref.shape[1]
    pool_dim = wpl_ref.shape[1]
    gdim = pool_dim // len(POOL_WINDOWS)
    conv_state = conv_width - 1
    pool_state = max(POOL_WINDOWS) - 1

    @pl.when(t == 0)
    def _():
        ucat[0:conv_hist, :] = jnp.zeros((conv_hist, conv_dim), F32)
        pcat[0:pool_hist, :] = jnp.zeros((pool_hist, pool_dim), F32)
        kcat[0:BLOCK, :] = jnp.zeros((BLOCK, LANES), F32)
        vcat[0:BLOCK, :] = jnp.zeros((BLOCK, LANES), F32)

    x = x_ref[...]
    xb = x.astype(BF16)

    hc = _dot(xb, wcv_ref[...])
    yield True
    u = hc[:, :conv_dim] * jax.nn.sigmoid(hc[:, conv_dim:])
    ucat[conv_hist:conv_hist + tq, :] = u
    gate_logits = _dot(xb, wg_ref[...])
    yield True
    acc = jnp.zeros((tq, conv_dim), F32) + convb_ref[...]
    base = conv_hist - conv_state
    window = ucat[...]
    n_rows = window.shape[0]
    rotated = {0: window}
    for j in range(conv_width):
        start, shift = (base + j) // SUBLANES * SUBLANES, (base + j) % SUBLANES
        if shift not in rotated:
            rotated[shift] = pltpu.roll(window, n_rows - shift, axis=0)
        acc = acc + convw_ref[j:j + 1, :] * rotated[shift][start:start + tq]
    hp = _dot(xb, wpl_ref[...])
    q = _dot(xb, wq_ref[...])
    kv = _dot(xb, wkv_ref[...])
    yield True
    c = _layer_norm(acc, clng_ref[...], clnb_ref[...])
    c = c * jax.nn.sigmoid(c)
    br_a = _dot(c.astype(BF16), wa_ref[...])
    oconv_ref[...] = ucat[conv_hist + tq - conv_state:conv_hist + tq, :]
    ucat[0:conv_hist, :] = ucat[tq:tq + conv_hist, :]
    yield True

    pcat[pool_hist:pool_hist + tq, :] = hp
    pos = t * tq + lax.broadcasted_iota(jnp.int32, (tq, gdim), 0)
    pooled = []
    for g, w in enumerate(POOL_WINDOWS):
        cols = slice(g * gdim, (g + 1) * gdim)
        s = pcat[pool_hist:pool_hist + tq, cols]
        cur = s
        for j in range(1, w):
            s = s + pcat[pool_hist - j:pool_hist - j + tq, cols]
        cnt = jnp.minimum(w, pos + 1).astype(F32)
        pooled.append(s / cnt - cur)
    br_b = _dot(_pool_mix(pooled, poolw_ref, pscale_ref).astype(BF16), wb_ref[...])
    opool_ref[...] = pcat[pool_hist + tq - pool_state:pool_hist + tq, :]
    pcat[0:pool_hist, :] = pcat[tq:tq + pool_hist, :]
    yield True

    kcat[BLOCK:BLOCK + tq, :] = kv[:, :LANES]
    vcat[BLOCK:BLOCK + tq, :] = kv[:, LANES:]
    sinks = [sinks_ref[h] for h in range(N_HEADS)]
    qi = lax.broadcasted_iota(jnp.int32, (BLOCK, 2 * BLOCK), 0)
    kj = lax.broadcasted_iota(jnp.int32, (BLOCK, 2 * BLOCK), 1)
    dist_i = qi + BLOCK - kj
    dist = dist_i.astype(F32)
    in_window = (dist_i >= 0) & (dist_i <= WINDOW)
    first_key = jnp.where(t == 0, BLOCK, 0)
    out_rest = jnp.where(in_window, 0.0, -NEG_BIG)
    out_first = jnp.where(in_window & (kj >= first_key), 0.0, -NEG_BIG)
    sub_rest = [_slope(h) * dist + out_rest for h in range(N_HEADS)]
    sub_first = [_slope(h) * dist + out_first for h in range(N_HEADS)]
    att_blocks = []
    for b in range(tq // BLOCK):
        sub = sub_first if b == 0 else sub_rest
        kblk = kcat[b * BLOCK:(b + 2) * BLOCK, :].astype(BF16)
        vblk = vcat[b * BLOCK:(b + 2) * BLOCK, :].astype(BF16)
        qblk = q[b * BLOCK:(b + 1) * BLOCK]
        cols = []
        for kvh in range(N_KV_HEADS):
            heads = [kvh * GROUP + a for a in range(GROUP)]
            q_pairs = [qblk[:, (kvh * GROUP + 2 * p) * HEAD_DIM:(kvh * GROUP + 2 * p + 2) * HEAD_DIM]
                       for p in range(GROUP // 2)]
            cols += _attention_group(q_pairs, _dup_lanes(kblk, kvh), _dup_lanes(vblk, kvh),
                                     sub.__getitem__, None, sinks, heads)
        att_blocks.append(jnp.concatenate(cols, axis=-1))
        yield True
    att = jnp.concatenate(att_blocks, axis=0)
    br_c = _dot(att.astype(BF16), wc_ref[...])
    ok_ref[...] = kcat[tq:tq + BLOCK, :]
    ov_ref[...] = vcat[tq:tq + BLOCK, :]
    kcat[0:BLOCK, :] = kcat[tq:tq + BLOCK, :]
    vcat[0:BLOCK, :] = vcat[tq:tq + BLOCK, :]
    yield True

    y_ref[...] = _merge_and_norm(x, gate_logits, br_a, br_b, br_c, wout_ref, ln1g_ref, ln1b_ref, alpha)


def _const_spec(arr):
    nd = arr.ndim
    return pl.BlockSpec(arr.shape, lambda *_: (0,) * nd, pipeline_mode=pl.Buffered(1))


def _prompt_mixer(x, sinks, wts, alpha, tq):
    batch, seq, d_model = x.shape
    conv_width, conv_dim = wts['conv_w'].shape
    pool_dim = wts['w_pl'].shape[1]
    conv_state = conv_width - 1
    pool_state = max(POOL_WINDOWS) - 1
    conv_hist = -(-conv_state // SUBLANES) * SUBLANES
    pool_hist = -(-pool_state // SUBLANES) * SUBLANES
    group = math.gcd(batch, PROMPT_SEQS_PER_STEP)
    assert seq % tq == 0 and tq % BLOCK == 0 and tq >= max(conv_hist, pool_hist, BLOCK)
    names = ['w_cv', 'w_pl', 'w_q', 'w_kv', 'w_g', 'conv_w', 'conv_b', 'conv_ln_g', 'conv_ln_b', 'w_a',
             'pool_w', 'pool_scale', 'w_b', 'w_c', 'w_out', 'ln1_g', 'ln1_b']
    consts = [wts[n] for n in names]
    kern = functools.partial(_prompt_mixer_kernel, tq=tq, conv_hist=conv_hist, pool_hist=pool_hist,
                             alpha=alpha)
    out_shapes = (
        jax.ShapeDtypeStruct((batch, seq, d_model), F32),
        jax.ShapeDtypeStruct((batch, conv_state, conv_dim), F32),
        jax.ShapeDtypeStruct((batch, pool_state, pool_dim), F32),
        jax.ShapeDtypeStruct((batch, WINDOW, LANES), F32),
        jax.ShapeDtypeStruct((batch, WINDOW, LANES), F32),
    )
    return pl.pallas_call(
        kern,
        grid=(batch // group, seq // tq),
        in_specs=[pl.BlockSpec(memory_space=pltpu.SMEM),
                  pl.BlockSpec((group, tq, d_model), lambda b, t: (b, t, 0))]
                 + [_const_spec(a) for a in consts],
        out_specs=(
            pl.BlockSpec((group, tq, d_model), lambda b, t: (b, t, 0)),
            pl.BlockSpec((group, conv_state, conv_dim), lambda b, t: (b, 0, 0)),
            pl.BlockSpec((group, pool_state, pool_dim), lambda b, t: (b, 0, 0)),
            pl.BlockSpec((group, WINDOW, LANES), lambda b, t: (b, 0, 0)),
            pl.BlockSpec((group, WINDOW, LANES), lambda b, t: (b, 0, 0)),
        ),
        out_shape=out_shapes,
        scratch_shapes=[
            pltpu.VMEM((group, conv_hist + tq, conv_dim), F32),
            pltpu.VMEM((group, pool_hist + tq, pool_dim), F32),
            pltpu.VMEM((group, BLOCK + tq, LANES), F32),
            pltpu.VMEM((group, BLOCK + tq, LANES), F32),
        ],
        compiler_params=pltpu.CompilerParams(
            dimension_semantics=("arbitrary", "arbitrary"), vmem_limit_bytes=VMEM_LIMIT_BYTES),
        name="prompt_mixer",
    )(sinks, x, *consts)


def _sample_mixer_kernel(sinks_ref, x_ref, cconv_ref, cpool_ref, ck_ref, cv_ref,
                         wcv_ref, wpl_ref, wq_ref, wkv_ref, wg_ref,
                         convw_ref, convb_ref, clng_ref, clnb_ref, wa_ref,
                         poolw_ref, pscale_ref, wb_ref, wc_ref, wout_ref, ln1g_ref, ln1b_ref,
                         y_ref, ou_ref, op_ref, ok_ref, ov_ref, *, alpha):
    n_new, nb, d_model = x_ref.shape
    conv_width, conv_dim = convw_ref.shape
    conv_state = conv_width - 1
    pool_dim = wpl_ref.shape[1]
    gdim = pool_dim // len(POOL_WINDOWS)
    pool_state = max(POOL_WINDOWS) - 1
    win_cache = ck_ref.shape[1]
    rows = n_new * nb

    x = x_ref[...].reshape(rows, d_model)
    xb = x.astype(BF16)

    hc = _dot(xb, wcv_ref[...])
    u = hc[:, :conv_dim] * jax.nn.sigmoid(hc[:, conv_dim:])
    u_l = [u[l * nb:(l + 1) * nb] for l in range(n_new)]
    conv_rows = []
    for l in range(n_new):
        acc = jnp.zeros((nb, conv_dim), F32) + convb_ref[...]
        for j in range(conv_width):
            i = l + j
            src = cconv_ref[i] if i < conv_state else u_l[i - conv_state]
            acc = acc + convw_ref[j:j + 1, :] * src
        conv_rows.append(acc)
    c = _layer_norm(jnp.concatenate(conv_rows, axis=0), clng_ref[...], clnb_ref[...])
    c = c * jax.nn.sigmoid(c)
    br_a = _dot(c.astype(BF16), wa_ref[...])
    ou_ref[...] = u.reshape(n_new, nb, conv_dim)

    hp = _dot(xb, wpl_ref[...])
    hp_l = [hp[l * nb:(l + 1) * nb] for l in range(n_new)]
    pooled_rows = []
    for l in range(n_new):
        groups = []
        for g, w in enumerate(POOL_WINDOWS):
            cols = slice(g * gdim, (g + 1) * gdim)
            s = hp_l[l][:, cols]
            for j in range(1, w):
                i = l - j
                s = s + (hp_l[i][:, cols] if i >= 0 else cpool_ref[pool_state + i][:, cols])
            cnt = float(min(w, PAST_LEN + l + 1))
            groups.append(s / cnt - hp_l[l][:, cols])
        pooled_rows.append(groups)
    pooled = [jnp.concatenate([pooled_rows[l][g] for l in range(n_new)], axis=0)
              for g in range(len(POOL_WINDOWS))]
    br_b = _dot(_pool_mix(pooled, poolw_ref, pscale_ref).astype(BF16), wb_ref[...])
    op_ref[...] = hp.reshape(n_new, nb, pool_dim)

    q = _dot(xb, wq_ref[...])
    kv = _dot(xb, wkv_ref[...])
    k_new = kv[:, :LANES]
    v_new = kv[:, LANES:]
    ok_ref[...] = k_new.reshape(n_new, nb, LANES)
    ov_ref[...] = v_new.reshape(n_new, nb, LANES)
    n_cached = nb * win_cache
    pad = jnp.zeros((LANES - rows, LANES), BF16)
    k_all = jnp.concatenate([ck_ref[...].reshape(n_cached, LANES).astype(BF16), k_new.astype(BF16), pad], axis=0)
    v_all = jnp.concatenate([cv_ref[...].reshape(n_cached, LANES).astype(BF16), v_new.astype(BF16), pad], axis=0)
    n_keys = n_cached + LANES
    r = lax.broadcasted_iota(jnp.int32, (rows, n_keys), 0)
    col = lax.broadcasted_iota(jnp.int32, (rows, n_keys), 1)
    q_l = r // nb
    q_b = r % nb
    cached = col < n_cached
    cn = col - n_cached
    k_b = jnp.where(cached, col // win_cache, cn % nb)
    dist_i = jnp.where(cached, q_l + win_cache - col % win_cache, q_l - cn // nb)
    valid = (k_b == q_b) & (dist_i >= 0) & (dist_i <= WINDOW) & (cached | (cn < rows))
    dist = dist_i.astype(F32)
    sinks = [sinks_ref[h] for h in range(N_HEADS)]
    cols = []
    for kvh in range(N_KV_HEADS):
        heads = [kvh * GROUP + a for a in range(GROUP)]
        q_pairs = [q[:, (kvh * GROUP + 2 * p) * HEAD_DIM:(kvh * GROUP + 2 * p + 2) * HEAD_DIM]
                   for p in range(GROUP // 2)]
        cols += _attention_group(q_pairs, _dup_lanes(k_all, kvh), _dup_lanes(v_all, kvh),
                                 lambda h: _slope(h) * dist, valid, sinks, heads)
    att = jnp.concatenate(cols, axis=-1)
    br_c = _dot(att.astype(BF16), wc_ref[...])

    y = _merge_and_norm(x, _dot(xb, wg_ref[...]), br_a, br_b, br_c, wout_ref, ln1g_ref, ln1b_ref, alpha)
    y_ref[...] = y.reshape(n_new, nb, d_model)


def _sample_mixer(x_lb, cconv_t, cpool_t, ck, cv, sinks, wts, alpha, nb):
    n_new, n_seq, d_model = x_lb.shape
    conv_state, _, conv_dim = cconv_t.shape
    pool_state, _, pool_dim = cpool_t.shape
    win_cache = ck.shape[1]
    assert n_seq % nb == 0 and nb % SUBLANES == 0 and n_new * nb <= LANES
    names = ['w_cv', 'w_pl', 'w_q', 'w_kv', 'w_g', 'conv_w', 'conv_b', 'conv_ln_g', 'conv_ln_b', 'w_a',
             'pool_w', 'pool_scale', 'w_b', 'w_c', 'w_out', 'ln1_g', 'ln1_b']
    consts = [wts[n] for n in names]

    def lb_spec(rows, width):
        return pl.BlockSpec((rows, nb, width), lambda c: (0, c, 0))

    out_shapes = (
        jax.ShapeDtypeStruct((n_new, n_seq, d_model), F32),
        jax.ShapeDtypeStruct((n_new, n_seq, conv_dim), F32),
        jax.ShapeDtypeStruct((n_new, n_seq, pool_dim), F32),
        jax.ShapeDtypeStruct((n_new, n_seq, LANES), F32),
        jax.ShapeDtypeStruct((n_new, n_seq, LANES), F32),
    )
    return pl.pallas_call(
        functools.partial(_sample_mixer_kernel, alpha=alpha),
        grid=(n_seq // nb,),
        in_specs=[pl.BlockSpec(memory_space=pltpu.SMEM),
                  lb_spec(n_new, d_model), lb_spec(conv_state, conv_dim), lb_spec(pool_state, pool_dim),
                  pl.BlockSpec((nb, win_cache, LANES), lambda c: (c, 0, 0)),
                  pl.BlockSpec((nb, win_cache, LANES), lambda c: (c, 0, 0))]
                 + [_const_spec(a) for a in consts],
        out_specs=(lb_spec(n_new, d_model), lb_spec(n_new, conv_dim), lb_spec(n_new, pool_dim),
                   lb_spec(n_new, LANES), lb_spec(n_new, LANES)),
        out_shape=out_shapes,
        compiler_params=pltpu.CompilerParams(
            dimension_semantics=("arbitrary",), vmem_limit_bytes=VMEM_LIMIT_BYTES),
        name="sample_mixer",
    )(sinks, x_lb, cconv_t, cpool_t, ck, cv, *consts)


def _route(x, wr_hi_ref, wr_lo_ref, rbias_ref):
    n_experts = wr_hi_ref.shape[0]
    per_group = n_experts // N_EXPERT_GROUPS
    tm = x.shape[0]
    x_hi = x.astype(BF16)
    x_lo = (x - x_hi.astype(F32)).astype(BF16)
    logits = (_dot_nt(wr_hi_ref[...], x_hi) + _dot_nt(wr_lo_ref[...], x_hi)
              + _dot_nt(wr_hi_ref[...], x_lo))
    scores = jax.nn.sigmoid(logits)
    sel = scores + rbias_ref[...]
    sub = lax.broadcasted_iota(jnp.int32, (per_group, tm), 0)
    grp_scores = []
    for g in range(N_EXPERT_GROUPS):
        v = sel[g * per_group:(g + 1) * per_group]
        m1 = jnp.max(v, axis=0, keepdims=True)
        first = jnp.min(jnp.where(v == m1, sub, per_group), axis=0, keepdims=True)
        m2 = jnp.max(jnp.where(sub == first, -jnp.inf, v), axis=0, keepdims=True)
        grp_scores.append(m1 + m2)
    keep = []
    for g in range(N_EXPERT_GROUPS):
        rank = jnp.zeros((1, tm), jnp.int32)
        for o in range(N_EXPERT_GROUPS):
            if o == g:
                continue
            ahead = (grp_scores[o] >= grp_scores[g]) if o < g else (grp_scores[o] > grp_scores[g])
            rank = rank + ahead.astype(jnp.int32)
        keep.append(jnp.broadcast_to(rank, (per_group, tm)))
    cand = jnp.where(jnp.concatenate(keep, axis=0) < TOPK_GROUPS, sel, -jnp.inf)
    eidx = lax.broadcasted_iota(jnp.int32, (n_experts, tm), 0)
    chosen = jnp.zeros((n_experts, tm), F32)
    picks, weights = [], []
    for _ in range(TOP_K):
        m = jnp.max(cand, axis=0, keepdims=True)
        first = jnp.min(jnp.where(cand == m, eidx, n_experts), axis=0, keepdims=True)
        pick = eidx == first
        picks.append(first)
        weights.append(jnp.sum(jnp.where(pick, scores, 0.0), axis=0, keepdims=True))
        chosen = jnp.where(pick, 1.0, chosen)
        cand = jnp.where(pick, -jnp.inf, cand)
    total = weights[0]
    for w in weights[1:]:
        total = total + w
    gates = [w / total * ROUTED_SCALE for w in weights]
    return picks, gates, chosen


def _plan_kernel(x_ref, wrh_ref, wrl_ref, rbias_ref, gates_ref, pos_ref, cnt_ref, off_ref):
    n_experts = wrh_ref.shape[0]
    tm = x_ref.shape[0]
    sub = min(tm, PLAN_SUB_TOKENS)
    picks, gates, chosen = _route(x_ref[...], wrh_ref, wrl_ref, rbias_ref)
    chosen_b = chosen.astype(BF16)
    before = (lax.broadcasted_iota(jnp.int32, (sub, sub), 0)
              < lax.broadcasted_iota(jnp.int32, (sub, sub), 1)).astype(BF16)
    ranks = []
    cnt = jnp.zeros((n_experts, 1), F32)
    for b in range(tm // sub):
        ranks.append(_dot(chosen_b[:, b * sub:(b + 1) * sub], before) + cnt)
        cnt = cnt + jnp.sum(chosen[:, b * sub:(b + 1) * sub], axis=1, keepdims=True)
    rank_in_expert = jnp.concatenate(ranks, axis=1)
    lower = (lax.broadcasted_iota(jnp.int32, (n_experts, n_experts), 1)
             < lax.broadcasted_iota(jnp.int32, (n_experts, n_experts), 0)).astype(BF16)
    off = jnp.sum(_dot(lower, chosen_b), axis=1, keepdims=True)
    rank = off + rank_in_expert
    eidx = lax.broadcasted_iota(jnp.int32, (n_experts, tm), 0)
    pos = [jnp.sum(jnp.where(eidx == p, rank, 0.0), axis=0, keepdims=True) for p in picks]
    for k in range(TOP_K):
        gates_ref[k] = gates[k]
        pos_ref[k] = (pos[k] * SLOT_ROWS).astype(jnp.int32)
    cnt_ref[0] = jnp.broadcast_to(cnt, (n_experts, LANES)).astype(jnp.int32)
    off_ref[0] = jnp.broadcast_to(off, (n_experts, LANES)).astype(jnp.int32)


def _moe_plan(x, wts, chunk):
    n_tok, d_model = x.shape
    n_experts = wts['wr_hi'].shape[0]
    n_chunks = n_tok // chunk
    consts = [wts['wr_hi'], wts['wr_lo'], wts['router_bias']]
    gates, pos, cnt, off = pl.pallas_call(
        _plan_kernel,
        grid=(n_chunks,),
        in_specs=[pl.BlockSpec((chunk, d_model), lambda c: (c, 0))] + [_const_spec(a) for a in consts],
        out_specs=(pl.BlockSpec((TOP_K, 1, chunk), lambda c: (0, 0, c)),
                   pl.BlockSpec((TOP_K, 1, chunk), lambda c: (0, 0, c)),
                   pl.BlockSpec((1, n_experts, LANES), lambda c: (c, 0, 0)),
                   pl.BlockSpec((1, n_experts, LANES), lambda c: (c, 0, 0))),
        out_shape=(jax.ShapeDtypeStruct((TOP_K, 1, n_tok), F32),
                   jax.ShapeDtypeStruct((TOP_K, 1, n_tok), jnp.int32),
                   jax.ShapeDtypeStruct((n_chunks, n_experts, LANES), jnp.int32),
                   jax.ShapeDtypeStruct((n_chunks, n_experts, LANES), jnp.int32)),
        compiler_params=pltpu.CompilerParams(
            dimension_semantics=("arbitrary",), vmem_limit_bytes=VMEM_LIMIT_BYTES),
        name="moe_plan",
    )(x, *consts)
    seg = jnp.concatenate([cnt[:, :, 0], off[:, :, 0]], axis=1)
    return gates, pos, seg.reshape(-1)


def _expert_kernel(*refs, tokens, row_tile, experts_per_step, block_rows, alpha):
    pos_refs, gate_refs = refs[:TOP_K], refs[TOP_K:2 * TOP_K]
    (seg_ref, x_hbm, weg_ref, weu_ref, wed_ref, wsg_ref, wsu_ref, wsd_ref, g_ref, b_ref,
     y_hbm, sbuf, pbuf, top_buf, bot_buf, x_stage, x_sem, y_stage, y_sem) = refs[2 * TOP_K:]
    step = pl.program_id(1)
    d_model = x_hbm.shape[1]
    half = d_model // 2
    n_experts = seg_ref.shape[0] // 2
    pairs = tokens * TOP_K
    n_blocks = tokens // block_rows
    chunk_row = pl.program_id(0) * tokens

    def hbm_rows(r):
        return pl.ds(pl.multiple_of(chunk_row + r * block_rows, block_rows), block_rows)

    def x_copy(r, buf):
        return pltpu.make_async_copy(x_hbm.at[hbm_rows(r), :], x_stage.at[buf], x_sem.at[buf])

    def staged_x(r):
        buf = lax.rem(r, STAGE_BUFFERS)
        x_copy(r, buf).wait()

        @pl.when(r + 1 < n_blocks)
        def _():
            x_copy(r + 1, lax.rem(r + 1, STAGE_BUFFERS)).start()
        return x_stage.at[buf]

    def pack(a, b):
        return pltpu.pack_elementwise([a, b], packed_dtype=BF16)

    def unpack(w):
        return tuple(pltpu.unpack_elementwise(w, index=i, packed_dtype=BF16, unpacked_dtype=F32)
                     for i in range(2))

    def slot_at(ref, row):
        return ref.at[pl.ds(pl.multiple_of(row, SLOT_ROWS), SLOT_ROWS), :]

    def slabs(ref, first_row, rows):
        return [ref.at[pl.ds(first_row + j, rows, stride=SLOT_ROWS), :] for j in range(SLOT_ROWS)]

    @pl.when(step == 0)
    def _():
        sbuf[pairs * SLOT_ROWS:, :] = jnp.zeros((row_tile * SLOT_ROWS, LANES), jnp.uint32)
        x_copy(0, 0).start()

        def scatter_block(r, carry):
            x_blk = staged_x(r)
            words = pack(x_blk[:, :half], x_blk[:, half:])
            for j, slab in enumerate(slabs(pbuf, 0, block_rows)):
                slab[...] = words[:, j * LANES:(j + 1) * LANES]

            def scatter(i, carry):
                for u in range(SCATTER_UNROLL):
                    local = i * SCATTER_UNROLL + u
                    t = r * block_rows + local
                    slot = slot_at(pbuf, local * SLOT_ROWS)[...]
                    for k in range(TOP_K):
                        slot_at(sbuf, pos_refs[k][0, 0, t])[...] = slot
                return carry
            return lax.fori_loop(0, block_rows // SCATTER_UNROLL, scatter, carry)
        lax.fori_loop(0, n_blocks, scatter_block, 0)

    def unpack_rows(words):
        return jnp.concatenate(unpack(words), axis=-1)

    def hidden(words, j):
        lb = unpack_rows(words).astype(BF16)
        return (jax.nn.silu(_dot(lb, weg_ref[j])) * _dot(lb, weu_ref[j])).astype(BF16)

    def expert_rows(words, j):
        return _dot(hidden(words, j), wed_ref[j])

    def load_tile(base):
        return jnp.concatenate([s[...] for s in slabs(sbuf, base, row_tile)], axis=-1)

    def store_tile(base, words, y, n_live):
        live = lax.broadcasted_iota(jnp.int32, y.shape, 0) < n_live
        out = jnp.where(live, y, unpack_rows(words))
        new_words = pack(out[:, :half], out[:, half:])
        for j, slab in enumerate(slabs(sbuf, base, row_tile)):
            slab[...] = new_words[:, j * LANES:(j + 1) * LANES]

    experts = [step * experts_per_step + j for j in range(experts_per_step)]
    n_rows = [seg_ref[ex] for ex in experts]
    bases = [pl.multiple_of(seg_ref[n_experts + ex] * SLOT_ROWS, SLOT_ROWS) for ex in experts]
    lhs = [load_tile(b) for b in bases]
    hs = [hidden(l, j) for j, l in enumerate(lhs)]
    ys = [_dot(h, wed_ref[j]) for j, h in enumerate(hs)]
    for j in range(experts_per_step):
        store_tile(bases[j], lhs[j], ys[j], n_rows[j])

    for j in range(experts_per_step):
        def run_tile(i, carry, j=j):
            base = pl.multiple_of(bases[j] + i * (row_tile * SLOT_ROWS), SLOT_ROWS)
            tile = load_tile(base)
            store_tile(base, tile, expert_rows(tile, j), n_rows[j] - i * row_tile)
            return carry
        lax.fori_loop(1, (n_rows[j] + row_tile - 1) // row_tile, run_tile, 0)

    @pl.when(step == pl.num_programs(1) - 1)
    def _():
        def y_copy(r, buf):
            return pltpu.make_async_copy(y_stage.at[buf], y_hbm.at[hbm_rows(r), :], y_sem.at[buf])

        x_copy(0, 0).start()

        def finish(r, carry):
            buf = lax.rem(r, STAGE_BUFFERS)

            @pl.when(r >= STAGE_BUFFERS)
            def _():
                y_copy(r - STAGE_BUFFERS, buf).wait()

            def combine(i, carry):
                for u in range(COMBINE_UNROLL):
                    local = i * COMBINE_UNROLL + u
                    t = r * block_rows + local
                    top = jnp.zeros((SLOT_ROWS, LANES), F32)
                    bot = jnp.zeros((SLOT_ROWS, LANES), F32)
                    for k in range(TOP_K):
                        a, b = unpack(slot_at(sbuf, pos_refs[k][0, 0, t])[...])
                        gate = gate_refs[k][0, 0, t]
                        top = top + gate * a
                        bot = bot + gate * b
                    slot_at(top_buf, local * SLOT_ROWS)[...] = top
                    slot_at(bot_buf, local * SLOT_ROWS)[...] = bot
                return carry
            lax.fori_loop(0, block_rows // COMBINE_UNROLL, combine, 0)

            routed = jnp.concatenate([s[...] for half_buf in (top_buf, bot_buf)
                                      for s in slabs(half_buf, 0, block_rows)], axis=-1)
            x = staged_x(r)[...]
            xb = x.astype(BF16)
            hs = jax.nn.silu(_dot(xb, wsg_ref[...])) * _dot(xb, wsu_ref[...])
            shared = _dot(hs.astype(BF16), wsd_ref[...])
            y_stage[buf] = _layer_norm(alpha * x + (routed + shared), g_ref[...], b_ref[...])
            y_copy(r, buf).start()
            return carry
        lax.fori_loop(0, n_blocks, finish, 0)
        for r in range(max(n_blocks - STAGE_BUFFERS, 0), n_blocks):
            y_copy(r, r % STAGE_BUFFERS).wait()


def _expert_row_tile(chunk, n_experts):
    mean = chunk * TOP_K / n_experts
    rows = mean + 3.0 * math.sqrt(mean * (1.0 - TOP_K / n_experts))
    return int(-(-rows // BF16_ROWS) * BF16_ROWS)


def _moe(x, wts, alpha, chunk):
    n_tok, d_model = x.shape
    n_experts, _, expert_dim = wts['w_e_gate'].shape
    assert d_model == 2 * SLOT_ROWS * LANES and n_tok % chunk == 0 and n_experts % EXPERTS_PER_STEP == 0
    gates, pos, seg = _moe_plan(x, wts, chunk)
    row_tile = _expert_row_tile(chunk, n_experts)
    block_rows = _pick_tile(chunk, 256)
    consts = [wts['w_s_gate'], wts['w_s_up'], wts['w_s_down'], wts['ln2_g'], wts['ln2_b']]
    once = pl.Buffered(1)
    kern = functools.partial(_expert_kernel, tokens=chunk, row_tile=row_tile,
                             experts_per_step=EXPERTS_PER_STEP, block_rows=block_rows, alpha=alpha)

    def w_spec(rows, cols):
        return pl.BlockSpec((EXPERTS_PER_STEP, rows, cols), lambda c, s: (s, 0, 0))

    def pick_spec(k):
        return pl.BlockSpec((1, 1, chunk), lambda c, s: (k, 0, c), memory_space=pltpu.SMEM, pipeline_mode=once)

    picks = [pick_spec(k) for k in range(TOP_K)]
    return pl.pallas_call(
        kern,
        grid=(n_tok // chunk, n_experts // EXPERTS_PER_STEP),
        in_specs=picks + picks + [
                  pl.BlockSpec((2 * n_experts,), lambda c, s: (c,), memory_space=pltpu.SMEM, pipeline_mode=once),
                  pl.BlockSpec(memory_space=pl.ANY),
                  w_spec(d_model, expert_dim), w_spec(d_model, expert_dim), w_spec(expert_dim, d_model)]
                 + [pl.BlockSpec(a.shape, lambda c, s: (0, 0), pipeline_mode=once) for a in consts],
        out_specs=pl.BlockSpec(memory_space=pl.ANY),
        out_shape=jax.ShapeDtypeStruct((n_tok, d_model), F32),
        scratch_shapes=[pltpu.VMEM(((chunk * TOP_K + row_tile) * SLOT_ROWS, LANES), jnp.uint32),
                        pltpu.VMEM((block_rows * SLOT_ROWS, LANES), jnp.uint32),
                        pltpu.VMEM((block_rows * SLOT_ROWS, LANES), F32),
                        pltpu.VMEM((block_rows * SLOT_ROWS, LANES), F32),
                        pltpu.VMEM((STAGE_BUFFERS, block_rows, d_model), F32),
                        pltpu.SemaphoreType.DMA((STAGE_BUFFERS,)),
                        pltpu.VMEM((STAGE_BUFFERS, block_rows, d_model), F32),
                        pltpu.SemaphoreType.DMA((STAGE_BUFFERS,))],
        compiler_params=pltpu.CompilerParams(
            dimension_semantics=("arbitrary", "arbitrary"), vmem_limit_bytes=EXPERT_VMEM_LIMIT_BYTES),
        name="moe_experts",
    )(*[pos] * TOP_K, *[gates] * TOP_K, seg, x, wts['w_e_gate'], wts['w_e_up'], wts['w_e_down'], *consts)


def _layer_weights(l, w_in, conv_w, conv_b, conv_ln_g, conv_ln_b, w_a, pool_w, pool_scale, w_b, w_c, w_out,
                   ln1_g, ln1_b, w_router, router_bias, w_e_gate, w_e_up, w_e_down, w_s_gate, w_s_up,
                   w_s_down, ln2_g, ln2_b):
    conv_dim = conv_w.shape[-1]
    pool_dim = pool_scale.shape[-1]
    attn_dim = w_c.shape[1]
    kv_dim = N_KV_HEADS * HEAD_DIM
    o1 = 2 * conv_dim
    o2 = o1 + pool_dim
    o3 = o2 + attn_dim
    o5 = o3 + 2 * kv_dim
    wi = w_in[l]
    row = lambda v: v[l][None, :].astype(F32)
    wr_t = w_router[l].T
    wr_hi = wr_t.astype(BF16)
    return {
        'w_cv': wi[:, :o1].astype(BF16), 'w_pl': wi[:, o1:o2].astype(BF16), 'w_q': wi[:, o2:o3].astype(BF16),
        'w_kv': wi[:, o3:o5].astype(BF16), 'w_g': wi[:, o5:].astype(BF16),
        'conv_w': conv_w[l], 'conv_b': row(conv_b), 'conv_ln_g': row(conv_ln_g), 'conv_ln_b': row(conv_ln_b),
        'w_a': w_a[l].astype(BF16), 'pool_w': pool_w[l].astype(BF16), 'pool_scale': row(pool_scale),
        'w_b': w_b[l].astype(BF16), 'w_c': w_c[l].astype(BF16), 'w_out': w_out[l].astype(BF16),
        'ln1_g': row(ln1_g), 'ln1_b': row(ln1_b),
        'wr_hi': wr_hi, 'wr_lo': (wr_t - wr_hi.astype(F32)).astype(BF16),
        'router_bias': router_bias[l][:, None].astype(F32),
        'w_e_gate': w_e_gate[l].astype(BF16), 'w_e_up': w_e_up[l].astype(BF16),
        'w_e_down': w_e_down[l].astype(BF16),
        'w_s_gate': w_s_gate[l].astype(BF16), 'w_s_up': w_s_up[l].astype(BF16),
        'w_s_down': w_s_down[l].astype(BF16),
        'ln2_g': row(ln2_g), 'ln2_b': row(ln2_b),
    }


def _pick_tile(n, target):
    t = min(n, target)
    while n % t:
        t //= 2
    return t


def kernel(x_prompt, x_sample, cache_conv, cache_pool, cache_k, cache_v, w_in, conv_w, conv_b, conv_ln_g,
           conv_ln_b, w_a, pool_w, pool_scale, w_b, attn_sinks, w_c, w_out, ln1_g, ln1_b, w_router,
           router_bias, w_e_gate, w_e_up, w_e_down, w_s_gate, w_s_up, w_s_down, ln2_g, ln2_b):
    depth = w_in.shape[0]
    batch, seq, d_model = x_prompt.shape
    n_seq, n_new, _ = x_sample.shape
    win_cache = cache_k.shape[2]
    alpha = (2 * depth) ** 0.25
    tq = _pick_tile(seq, PROMPT_TILE_TOKENS)
    nb = _pick_tile(n_seq, LANES // n_new // 2)
    tm_prompt = _pick_tile(batch * seq, MOE_CHUNK_TOKENS)
    tm_sample = _pick_tile(n_seq * n_new, MOE_CHUNK_TOKENS)

    yp = x_prompt
    ys = jnp.transpose(x_sample, (1, 0, 2))
    outs = [[] for _ in range(8)]
    for l in range(depth):
        wts = _layer_weights(l, w_in, conv_w, conv_b, conv_ln_g, conv_ln_b, w_a, pool_w, pool_scale, w_b,
                             w_c, w_out, ln1_g, ln1_b, w_router, router_bias, w_e_gate, w_e_up, w_e_down,
                             w_s_gate, w_s_up, w_s_down, ln2_g, ln2_b)
        sinks = attn_sinks[l].astype(F32)

        yp, c1, p1, k1, v1 = _prompt_mixer(yp, sinks, wts, alpha, tq)
        yp = _moe(yp.reshape(batch * seq, d_model), wts, alpha, tm_prompt).reshape(batch, seq, d_model)

        ck = cache_k[l].reshape(n_seq, win_cache, LANES)
        cv = cache_v[l].reshape(n_seq, win_cache, LANES)
        ys, u2, hp2, k2, v2 = _sample_mixer(
            ys, jnp.transpose(cache_conv[l], (1, 0, 2)), jnp.transpose(cache_pool[l], (1, 0, 2)),
            ck, cv, sinks, wts, alpha, nb)
        ys = _moe(ys.reshape(n_new * n_seq, d_model), wts, alpha, tm_sample).reshape(n_new, n_seq, d_model)

        to_bl = lambda a: jnp.transpose(a, (1, 0, 2))
        outs[0].append(c1)
        outs[1].append(p1)
        outs[2].append(k1.reshape(batch, WINDOW, N_KV_HEADS, HEAD_DIM))
        outs[3].append(v1.reshape(batch, WINDOW, N_KV_HEADS, HEAD_DIM))
        outs[4].append(jnp.concatenate([cache_conv[l], to_bl(u2)], axis=1)[:, n_new:])
        outs[5].append(jnp.concatenate([cache_pool[l], to_bl(hp2)], axis=1)[:, n_new:])
        outs[6].append(jnp.concatenate([ck, to_bl(k2)], axis=1)[:, n_new:]
                       .reshape(n_seq, win_cache, N_KV_HEADS, HEAD_DIM))
        outs[7].append(jnp.concatenate([cv, to_bl(v2)], axis=1)[:, n_new:]
                       .reshape(n_seq, win_cache, N_KV_HEADS, HEAD_DIM))
    return (yp, jnp.transpose(ys, (1, 0, 2))) + tuple(jnp.stack(o) for o in outs)
```
